```python
import math
import jax, jax.numpy as jnp
from jax import lax
import numpy as np

D_MODEL = 1024
BATCH = 4
SEQ = 8192
DEPTH = 1

GRID_W = 64
CTX_LEN = 256
HEAD_DIM = 64
N_HEADS_DIFF = D_MODEL // (4 * HEAD_DIM)
N_HEADS_NA = D_MODEL // (2 * HEAD_DIM)
DIFF_WIDTH = N_HEADS_DIFF * 2 * HEAD_DIM
NA_WIDTH = N_HEADS_NA * HEAD_DIM
MIX_WIDTH = DIFF_WIDTH + NA_WIDTH
NA_ROWS_MAX = 8
NA_COLS = 16
ROPE_BASE = 10000.0
ROPE_FREQS = HEAD_DIM // 4
Q_BLOCK = 128
N_EXPERTS = 32
TOP_K = 4
D_FF = D_MODEL
SWIGLU_ALPHA = 1.702
SWIGLU_LIMIT = 7.0
EXPERT_BLOCK = 256
NORM_EPS = 1e-6

kernel_name = 'hybrid_diffattn_natten_moe_dit_layer'


def rms_norm(x, g):
    xf = x.astype(jnp.float32)
    y = xf * lax.rsqrt(jnp.mean(xf * xf, axis=-1, keepdims=True) + NORM_EPS)
    return (y * g.astype(jnp.float32)).astype(x.dtype)


def modulate(h, shift, scale):
    return h * (1.0 + scale) + shift


def adaln(cv, w_ada, b_ada):
    return (jax.nn.silu(cv) @ w_ada + b_ada).reshape(cv.shape[0], 6, -1)


def axial_rope_tables(row, col):
    inv = 1.0 / (ROPE_BASE ** (jnp.arange(ROPE_FREQS, dtype=jnp.float32) / ROPE_FREQS))
    ang_r = row.astype(jnp.float32)[:, None] * inv
    ang_c = col.astype(jnp.float32)[:, None] * inv
    return (jnp.cos(ang_r), jnp.sin(ang_r), jnp.cos(ang_c), jnp.sin(ang_c))


def _rotate(x, cos, sin):
    x1, x2 = x[..., :ROPE_FREQS], x[..., ROPE_FREQS:]
    return jnp.concatenate([x1 * cos - x2 * sin, x2 * cos + x1 * sin], axis=-1)


def apply_axial_rope(x, tables):
    cr, sr, cc, sc = [t.reshape((t.shape[0],) + (1,) * (x.ndim - 3) + (ROPE_FREQS,)) for t in tables]
    half = HEAD_DIM // 2
    out = jnp.concatenate([_rotate(x[..., :half], cr, sr), _rotate(x[..., half:], cc, sc)], axis=-1)
    return out.astype(x.dtype)


def project_heads(h, w_in, qn_d, kn_d, qn_n, kn_n):
    p = h @ w_in
    bn, n = p.shape[:2]
    cuts = [DIFF_WIDTH, 2 * DIFF_WIDTH, 3 * DIFF_WIDTH, 3 * DIFF_WIDTH + NA_WIDTH, 3 * DIFF_WIDTH + 2 * NA_WIDTH]
    qd, kd, vd, qn, kn, vn = jnp.split(p, cuts, axis=-1)
    qd = rms_norm(qd.reshape(bn, n, N_HEADS_DIFF, 2, HEAD_DIM), qn_d)
    kd = rms_norm(kd.reshape(bn, n, N_HEADS_DIFF, 2, HEAD_DIM), kn_d)
    vd = vd.reshape(bn, n, N_HEADS_DIFF, 2 * HEAD_DIM)
    qn = rms_norm(qn.reshape(bn, n, N_HEADS_NA, HEAD_DIM), qn_n)
    kn = rms_norm(kn.reshape(bn, n, N_HEADS_NA, HEAD_DIM), kn_n)
    vn = vn.reshape(bn, n, N_HEADS_NA, HEAD_DIM)
    return qd, kd, vd, qn, kn, vn


def diff_attend(q, k, v, lam):
    s = jnp.einsum('bqhmd,bkhmd->bhmqk', q, k).astype(jnp.float32) * (HEAD_DIM ** -0.5)
    p = jax.nn.softmax(s, axis=-1).astype(v.dtype)
    o = jnp.einsum('bhmqk,bkhe->bqhme', p, v)
    return o[:, :, :, 0] - lam.astype(o.dtype) * o[:, :, :, 1]


def diff_attention_blocks(q, k, v, lam):
    b, s = q.shape[:2]
    nb = s // Q_BLOCK
    qb = q.reshape(b, nb, Q_BLOCK, N_HEADS_DIFF, 2, HEAD_DIM).swapaxes(0, 1)
    out = lax.map(lambda qq: diff_attend(qq, k, v, lam), qb)
    return out.swapaxes(0, 1).reshape(b, s, N_HEADS_DIFF, 2 * HEAD_DIM)


def softmax_attention(q, k, v):
    s = jnp.einsum('bqhd,bkhd->bhqk', q, k).astype(jnp.float32) * (HEAD_DIM ** -0.5)
    p = jax.nn.softmax(s, axis=-1).astype(v.dtype)
    return jnp.einsum('bhqk,bkhd->bqhd', p, v)


def neighbourhood_attention(q, k, v, k_ctx, v_ctx, rpb, rows_n):
    b, s, h, dh = q.shape
    kr = min(NA_ROWS_MAX, rows_n)
    kc = NA_COLS
    qg = q.reshape(b, rows_n, GRID_W, h, dh).swapaxes(0, 1)
    kg = k.reshape(b, rows_n, GRID_W, h, dh)
    vg = v.reshape(b, rows_n, GRID_W, h, dh)
    cols = jnp.arange(GRID_W)
    col_start = jnp.clip(cols - kc // 2, 0, GRID_W - kc)
    col_idx = col_start[:, None] + jnp.arange(kc)[None, :]
    col_bias_idx = col_idx - cols[:, None] + (NA_COLS - 1)
    scale = HEAD_DIM ** -0.5

    def row_block(args):
        r, q_r = args
        r0 = jnp.clip(r - kr // 2, 0, rows_n - kr)
        k_rows = lax.dynamic_slice_in_dim(kg, r0, kr, axis=1)
        v_rows = lax.dynamic_slice_in_dim(vg, r0, kr, axis=1)
        k_win = k_rows[:, :, col_idx]
        v_win = v_rows[:, :, col_idx]
        row_bias_idx = r0 + jnp.arange(kr) - r + (NA_ROWS_MAX - 1)
        bias = rpb[:, row_bias_idx][:, :, col_bias_idx]
        s_lat = jnp.einsum('bwhd,brwkhd->bhwrk', q_r, k_win).astype(jnp.float32) * scale
        s_lat = s_lat + bias.transpose(0, 2, 1, 3)[None].astype(jnp.float32)
        s_ctx = jnp.einsum('bwhd,bchd->bhwc', q_r, k_ctx).astype(jnp.float32) * scale
        sc = jnp.concatenate([s_lat.reshape(b, h, GRID_W, kr * kc), s_ctx], axis=-1)
        p = jax.nn.softmax(sc, axis=-1).astype(v.dtype)
        p_lat = p[..., :kr * kc].reshape(b, h, GRID_W, kr, kc)
        p_ctx = p[..., kr * kc:]
        return (jnp.einsum('bhwrk,brwkhd->bwhd', p_lat, v_win)
                + jnp.einsum('bhwc,bchd->bwhd', p_ctx, v_ctx))

    out = lax.map(row_block, (jnp.arange(rows_n), qg))
    return out.swapaxes(0, 1).reshape(b, s, h, dh)


def routed_swiglu_moe(h, w_router, b_router, w_gate_up, b_gate_up, w_down, b_down):
    t, d = h.shape
    logits = (h @ w_router + b_router).astype(jnp.float32)
    top_vals, top_idx = lax.top_k(logits, TOP_K)
    gates = jax.nn.softmax(top_vals, axis=-1)
    n_assign = t * TOP_K
    e_flat = top_idx.reshape(-1)
    order = jnp.argsort(e_flat)
    e_sorted = e_flat[order]
    tok_sorted = order // TOP_K
    gate_sorted = gates.reshape(-1)[order]
    counts = jnp.bincount(e_flat, length=N_EXPERTS)
    starts = jnp.cumsum(counts) - counts
    padded = (counts + EXPERT_BLOCK - 1) // EXPERT_BLOCK * EXPERT_BLOCK
    pad_ends = jnp.cumsum(padded)
    pad_starts = pad_ends - padded
    dest = pad_starts[e_sorted] + jnp.arange(n_assign) - starts[e_sorted]
    n_blocks = -(-n_assign // EXPERT_BLOCK) + N_EXPERTS
    buf = jnp.zeros((n_blocks * EXPERT_BLOCK, d), h.dtype).at[dest].set(h[tok_sorted])
    block_expert = jnp.minimum(
        jnp.searchsorted(pad_ends, jnp.arange(n_blocks) * EXPERT_BLOCK, side='right'), N_EXPERTS - 1)

    def expert_block(args):
        xb, e = args
        gu = xb @ w_gate_up[e] + b_gate_up[e]
        glu = jnp.minimum(gu[:, :D_FF], SWIGLU_LIMIT)
        lin = jnp.clip(gu[:, D_FF:], -SWIGLU_LIMIT, SWIGLU_LIMIT)
        act = glu * jax.nn.sigmoid(SWIGLU_ALPHA * glu) * (lin + 1.0)
        return act @ w_down[e] + b_down[e]

    ys = lax.map(expert_block, (buf.reshape(n_blocks, EXPERT_BLOCK, d), block_expert)).reshape(-1, d)
    contrib = ys[dest] * gate_sorted[:, None].astype(ys.dtype)
    return jax.ops.segment_sum(contrib, tok_sorted, num_segments=t)


def hybrid_layer(x, ctx, c, c_ctx, lp, layer_idx, rope, rows_n, update_ctx):
    b, s, d = x.shape
    mod = adaln(c, lp['w_ada'], lp['b_ada'])
    mod_c = adaln(c_ctx[None, :], lp['w_ada'], lp['b_ada'])
    sh1, sc1, g1, sh2, sc2, g2 = [mod[:, i][:, None, :] for i in range(6)]
    sh1c, sc1c, g1c, sh2c, sc2c, g2c = [mod_c[:, i][:, None, :] for i in range(6)]

    h_lat = modulate(rms_norm(x, lp['g_attn']), sh1, sc1)
    h_ctx = modulate(rms_norm(ctx, lp['g_attn']), sh1c, sc1c)
    norms = (lp['q_norm_diff'], lp['k_norm_diff'], lp['q_norm_na'], lp['k_norm_na'])
    qd, kd, vd, qn, kn, vn = project_heads(h_lat, lp['w_in'], *norms)
    qd_c, kd_c, vd_c, qn_c, kn_c, vn_c = project_heads(h_ctx, lp['w_in'], *norms)
    qd = apply_axial_rope(qd, rope)
    kd = apply_axial_rope(kd, rope)

    lam_init = 0.8 - 0.6 * math.exp(-0.3 * layer_idx)
    lam = (jnp.exp(jnp.sum(lp['lam_q1'].astype(jnp.float32) * lp['lam_k1'].astype(jnp.float32)))
           - jnp.exp(jnp.sum(lp['lam_q2'].astype(jnp.float32) * lp['lam_k2'].astype(jnp.float32)))
           + lam_init)

    kd_all = jnp.concatenate([kd, kd_c], axis=1)
    vd_all = jnp.concatenate([vd, vd_c], axis=1)
    o_diff = rms_norm(diff_attention_blocks(qd, kd_all, vd_all, lam), lp['subln_diff']) * (1.0 - lam_init)
    o_na = neighbourhood_attention(qn, kn, vn, kn_c, vn_c, lp['rpb_na'], rows_n)
    o_na = rms_norm(o_na, lp['out_norm_na'].reshape(N_HEADS_NA, HEAD_DIM))
    mix = jnp.concatenate([o_diff.reshape(b, s, DIFF_WIDTH), o_na.reshape(b, s, NA_WIDTH)], axis=-1)
    x = x + g1 * (mix @ lp['w_out'])

    moe_w = (lp['w_router'], lp['b_router'], lp['w_gate_up'], lp['b_gate_up'], lp['w_down'], lp['b_down'])
    h2 = modulate(rms_norm(x, lp['g_ffn']), sh2, sc2)
    x = x + g2 * routed_swiglu_moe(h2.reshape(-1, d), *moe_w).reshape(b, s, d)

    if update_ctx:
        oc_diff = rms_norm(diff_attend(qd_c, kd_c, vd_c, lam), lp['subln_diff']) * (1.0 - lam_init)
        oc_na = rms_norm(softmax_attention(qn_c, kn_c, vn_c), lp['out_norm_na'].reshape(N_HEADS_NA, HEAD_DIM))
        bc, nc = ctx.shape[:2]
        mix_c = jnp.concatenate([oc_diff.reshape(bc, nc, DIFF_WIDTH), oc_na.reshape(bc, nc, NA_WIDTH)], axis=-1)
        ctx = ctx + g1c * (mix_c @ lp['w_out'])
        h2c = modulate(rms_norm(ctx, lp['g_ffn']), sh2c, sc2c)
        ctx = ctx + g2c * routed_swiglu_moe(h2c.reshape(-1, d), *moe_w).reshape(ctx.shape)
    return x, ctx


def setup_inputs(seed: int = 0) -> dict:
    key = jax.random.key(seed)
    ks = jax.random.split(key, 32)
    f32 = jnp.float32
    nrm = lambda k, shape, sc: jax.random.normal(k, shape, f32) * sc
    L, D, E, F = DEPTH, D_MODEL, N_EXPERTS, D_FF
    return {
        'x': nrm(ks[0], (BATCH, SEQ, D), 1.0),
        'c': nrm(ks[1], (BATCH, D), 1.0),
        'ctx': nrm(ks[2], (BATCH, CTX_LEN, D), 1.0),
        'c_ctx': nrm(ks[3], (D,), 1.0),
        'w_ada': nrm(ks[4], (L, D, 6 * D), 0.5 * D ** -0.5),
        'b_ada': nrm(ks[5], (L, 6 * D), 0.01),
        'g_attn': 1.0 + nrm(ks[6], (L, D), 0.05),
        'w_in': nrm(ks[7], (L, D, 3 * MIX_WIDTH), D ** -0.5),
        'q_norm_diff': 1.0 + nrm(ks[8], (L, HEAD_DIM), 0.05),
        'k_norm_diff': 1.0 + nrm(ks[9], (L, HEAD_DIM), 0.05),
        'lam_q1': nrm(ks[10], (L, HEAD_DIM), 0.1),
        'lam_k1': nrm(ks[11], (L, HEAD_DIM), 0.1),
        'lam_q2': nrm(ks[12], (L, HEAD_DIM), 0.1),
        'lam_k2': nrm(ks[13], (L, HEAD_DIM), 0.1),
        'subln_diff': 1.0 + nrm(ks[14], (L, 2 * HEAD_DIM), 0.05),
        'q_norm_na': 1.0 + nrm(ks[15], (L, HEAD_DIM), 0.05),
        'k_norm_na': 1.0 + nrm(ks[16], (L, HEAD_DIM), 0.05),
        'rpb_na': nrm(ks[17], (L, N_HEADS_NA, 2 * NA_ROWS_MAX - 1, 2 * NA_COLS - 1), 0.1),
        'out_norm_na': 1.0 + nrm(ks[18], (L, NA_WIDTH), 0.05),
        'w_out': nrm(ks[19], (L, MIX_WIDTH, D), MIX_WIDTH ** -0.5),
        'g_ffn': 1.0 + nrm(ks[20], (L, D), 0.05),
        'w_router': nrm(ks[21], (L, D, E), D ** -0.5),
        'b_router': nrm(ks[22], (L, E), 0.01),
        'w_gate_up': nrm(ks[23], (L, E, D, 2 * F), D ** -0.5),
        'b_gate_up': nrm(ks[24], (L, E, 2 * F), 0.01),
        'w_down': nrm(ks[25], (L, E, F, D), F ** -0.5),
        'b_down': nrm(ks[26], (L, E, D), 0.01),
    }


def reference(x, c, ctx, c_ctx, w_ada, b_ada, g_attn, w_in, q_norm_diff, k_norm_diff,
              lam_q1, lam_k1, lam_q2, lam_k2, subln_diff, q_norm_na, k_norm_na, rpb_na,
              out_norm_na, w_out, g_ffn, w_router, b_router, w_gate_up, b_gate_up, w_down, b_down):
    s = x.shape[1]
    rows_n = s // GRID_W
    pos = jnp.arange(s, dtype=jnp.int32)
    rope = axial_rope_tables(pos // GRID_W, pos % GRID_W)
    for l in range(DEPTH):
        lp = {
            'w_ada': w_ada[l], 'b_ada': b_ada[l], 'g_attn': g_attn[l], 'w_in': w_in[l],
            'q_norm_diff': q_norm_diff[l], 'k_norm_diff': k_norm_diff[l],
            'lam_q1': lam_q1[l], 'lam_k1': lam_k1[l], 'lam_q2': lam_q2[l], 'lam_k2': lam_k2[l],
            'subln_diff': subln_diff[l], 'q_norm_na': q_norm_na[l], 'k_norm_na': k_norm_na[l],
            'rpb_na': rpb_na[l], 'out_norm_na': out_norm_na[l], 'w_out': w_out[l], 'g_ffn': g_ffn[l],
            'w_router': w_router[l], 'b_router': b_router[l], 'w_gate_up': w_gate_up[l],
            'b_gate_up': b_gate_up[l], 'w_down': w_down[l], 'b_down': b_down[l],
        }
        x, ctx = hybrid_layer(x, ctx, c, c_ctx, lp, l, rope, rows_n, l < DEPTH - 1)
    return x
```

```python
import functools
import math

import jax
import jax.numpy as jnp
import numpy as np
from jax import lax
from jax.experimental import pallas as pl
from jax.experimental.pallas import tpu as pltpu

GRID_W = 64
HEAD_DIM = 64
NA_ROWS = 8
NA_COLS = 16
ROPE_BASE = 10000.0
ROPE_FREQS = HEAD_DIM // 4
N_EXPERTS = 32
TOP_K = 4
SWIGLU_ALPHA = 1.702
SWIGLU_LIMIT = 7.0
NORM_EPS = 1e-6

LANES = 128
NEG = -1e30
LOG2E = math.log2(math.e)
Q_SCALE = HEAD_DIM ** -0.5 * LOG2E

PROJ_TM = 512
DIFF_TQ = 256
NA_ROWS_PER_STEP = 16
ROUTE_TM = 512
EXPERT_TM = 512
MOVE_TM = 256
VMEM_LIMIT = 48 * 1024 * 1024

_NT = (((1,), (1,)), ((), ()))


def _f32(x):
    return x.astype(jnp.float32)


def _adaln_kernel(cv_ref, w_ref, b_ref, o_ref):
    cv = cv_ref[...]
    act = cv * (1.0 / (1.0 + jnp.exp(-cv)))
    o_ref[...] = jnp.dot(act, w_ref[...], precision=lax.Precision.HIGHEST,
                         preferred_element_type=jnp.float32) + b_ref[...]


def _adaln(cv, w_ada, b_ada):
    rows, d = cv.shape
    n = w_ada.shape[1]
    tn = 1024
    return pl.pallas_call(
        _adaln_kernel,
        out_shape=jax.ShapeDtypeStruct((rows, n), jnp.float32),
        grid=(n // tn,),
        in_specs=[pl.BlockSpec((rows, d), lambda j: (0, 0)),
                  pl.BlockSpec((d, tn), lambda j: (0, j)),
                  pl.BlockSpec((1, tn), lambda j: (0, j))],
        out_specs=pl.BlockSpec((rows, tn), lambda j: (0, j)),
        compiler_params=pltpu.CompilerParams(vmem_limit_bytes=VMEM_LIMIT),
        name="adaln",
    )(cv, w_ada, b_ada.reshape(1, n))


def _proj_kernel(*refs, groups, rope, d_model):
    (x_ref, g_ref, sh_ref, sc_ref, w_ref, gsum_ref, cos_ref, sin_ref,
     qnd_ref, knd_ref, qnn_ref, knn_ref) = refs[:12]
    outs = dict(zip(groups, refs[12:]))
    width = 4 * LANES

    x = x_ref[0]
    y = x * lax.rsqrt(jnp.mean(x * x, axis=-1, keepdims=True) + NORM_EPS) * g_ref[...]
    h = (y * (1.0 + sc_ref[0]) + sh_ref[0]).astype(jnp.bfloat16)

    group_col = {"qd": 0, "kd": 1, "vd": 2, "qn": 3, "kn": 4, "vn": 5}
    gains = {"qd": qnd_ref[...] * Q_SCALE, "kd": knd_ref[...],
             "qn": qnn_ref[...] * Q_SCALE, "kn": knn_ref[...]}
    lane = lax.broadcasted_iota(jnp.int32, (x.shape[0], LANES), 1)
    first_half = (lane % (2 * ROPE_FREQS)) < ROPE_FREQS

    for name in groups:
        c0 = group_col[name] * width
        p = jnp.dot(h, w_ref[:, c0:c0 + width], preferred_element_type=jnp.float32)
        if name in gains:
            blocks = []
            for j in range(width // 256):
                blk = p[:, j * 256:(j + 1) * 256]
                ss = jnp.dot((blk * blk).astype(jnp.bfloat16), gsum_ref[...],
                             preferred_element_type=jnp.float32)
                blocks.append(blk * lax.rsqrt(ss * (1.0 / HEAD_DIM) + NORM_EPS))
            p = jnp.concatenate(blocks, axis=1) * gains[name]
        if rope and name in ("qd", "kd"):
            blocks = []
            for j in range(width // LANES):
                blk = p[:, j * LANES:(j + 1) * LANES]
                partner = jnp.where(first_half,
                                    pltpu.roll(blk, LANES - ROPE_FREQS, axis=1),
                                    pltpu.roll(blk, ROPE_FREQS, axis=1))
                blocks.append(blk * cos_ref[...] + partner * sin_ref[...])
            p = jnp.concatenate(blocks, axis=1)
        if name == "vd":
            pt = p.T.reshape(width // LANES, LANES, p.shape[0])
            outs[name][0, :, 0] = pt.astype(jnp.bfloat16)
        else:
            outs[name][0] = p.astype(jnp.bfloat16)


def _proj(x, g_attn, shift, scale, w_in, gsum, cos_t, sin_t, norms, *, groups, rope, tm):
    b, s, d = x.shape
    nt = s // tm
    width = 4 * LANES
    out_shape, out_specs = [], []
    for name in groups:
        if name == "vd":
            out_shape.append(jax.ShapeDtypeStruct((b, 4, nt, LANES, tm), jnp.bfloat16))
            out_specs.append(pl.BlockSpec((1, 4, 1, LANES, tm), lambda bi, i: (bi, 0, i, 0, 0)))
        else:
            out_shape.append(jax.ShapeDtypeStruct((b, s, width), jnp.bfloat16))
            out_specs.append(pl.BlockSpec((1, tm, width), lambda bi, i: (bi, i, 0)))
    row = lambda bi, i: (0, 0)
    per_b = lambda bi, i: (bi, 0, 0)
    kern = functools.partial(_proj_kernel, groups=groups, rope=rope, d_model=d)
    return pl.pallas_call(
        kern,
        out_shape=out_shape,
        grid=(b, nt),
        in_specs=[pl.BlockSpec((1, tm, d), lambda bi, i: (bi, i, 0)),
                  pl.BlockSpec((1, d), row),
                  pl.BlockSpec((1, 1, d), per_b),
                  pl.BlockSpec((1, 1, d), per_b),
                  pl.BlockSpec(w_in.shape, row),
                  pl.BlockSpec(gsum.shape, row),
                  pl.BlockSpec((tm, LANES), lambda bi, i: (i, 0)),
                  pl.BlockSpec((tm, LANES), lambda bi, i: (i, 0)),
                  pl.BlockSpec((1, width), row), pl.BlockSpec((1, width), row),
                  pl.BlockSpec((1, width), row), pl.BlockSpec((1, width), row)],
        out_specs=out_specs,
        compiler_params=pltpu.CompilerParams(vmem_limit_bytes=VMEM_LIMIT),
        name="proj_rope" if rope else "proj_ctx",
    )(x, g_attn, shift, scale, w_in, gsum, cos_t, sin_t, *norms)


def _diff_attn_kernel(q_ref, k_ref, kc_ref, vt_ref, vtc_ref, lq1_ref, lk1_ref, lq2_ref, lk2_ref,
                      subln_ref, o_ref, m_ref, l_ref, acc_ref, *, lam_init, n_kblocks):
    q = q_ref[0]
    lane = lax.broadcasted_iota(jnp.int32, q.shape, 1)
    zero = jnp.zeros_like(q)
    qz = (jnp.where(lane < HEAD_DIM, q, zero), jnp.where(lane >= HEAD_DIM, q, zero))

    m_ref[...] = jnp.full(m_ref.shape, NEG, jnp.float32)
    l_ref[...] = jnp.zeros(l_ref.shape, jnp.float32)
    acc_ref[...] = jnp.zeros(acc_ref.shape, jnp.float32)

    def block(k, vt):
        for mp in range(2):
            s = lax.dot_general(k, qz[mp], _NT, preferred_element_type=jnp.float32)
            m_old = m_ref[mp]
            m_new = jnp.maximum(m_old, jnp.max(s, axis=0, keepdims=True))
            alpha = jnp.exp2(m_old - m_new)
            p = jnp.exp2(s - m_new)
            l_ref[mp] = alpha * l_ref[mp] + jnp.sum(p, axis=0, keepdims=True)
            acc_ref[mp] = alpha * acc_ref[mp] + jnp.dot(vt, p.astype(jnp.bfloat16),
                                                        preferred_element_type=jnp.float32)
            m_ref[mp] = m_new

    def body(i, carry):
        block(k_ref[0, i], vt_ref[0, 0, i])
        return carry

    lax.fori_loop(0, n_kblocks, body, 0)
    block(kc_ref[0], vtc_ref[0, 0, 0])

    lam = (jnp.exp(jnp.sum(lq1_ref[...] * lk1_ref[...], keepdims=True))
           - jnp.exp(jnp.sum(lq2_ref[...] * lk2_ref[...], keepdims=True)) + lam_init)
    o = acc_ref[0] / l_ref[0] - lam * (acc_ref[1] / l_ref[1])
    ot = o.T
    ot = ot * lax.rsqrt(jnp.mean(ot * ot, axis=-1, keepdims=True) + NORM_EPS)
    o_ref[0] = (ot * subln_ref[...] * (1.0 - lam_init)).astype(o_ref.dtype)


def _diff_attn(qd, kd, kd_c, vdt, vdt_c, lams, subln, *, lam_init):
    b, s, width = qd.shape
    heads = width // LANES
    tk = vdt.shape[-1]
    nkb = s // tk
    c = kd_c.shape[1]
    kd4 = kd.reshape(b, nkb, tk, width)
    tq = DIFF_TQ
    kern = functools.partial(_diff_attn_kernel, lam_init=lam_init, n_kblocks=nkb)
    vec = pl.BlockSpec((1, HEAD_DIM), lambda bi, h, i: (0, 0))
    return pl.pallas_call(
        kern,
        out_shape=jax.ShapeDtypeStruct((b, s, width), jnp.bfloat16),
        grid=(b, heads, s // tq),
        in_specs=[pl.BlockSpec((1, tq, LANES), lambda bi, h, i: (bi, i, h)),
                  pl.BlockSpec((1, nkb, tk, LANES), lambda bi, h, i: (bi, 0, 0, h)),
                  pl.BlockSpec((1, c, LANES), lambda bi, h, i: (bi, 0, h)),
                  pl.BlockSpec((1, 1, nkb, LANES, tk), lambda bi, h, i: (bi, h, 0, 0, 0)),
                  pl.BlockSpec((1, 1, 1, LANES, c), lambda bi, h, i: (bi, h, 0, 0, 0)),
                  vec, vec, vec, vec,
                  pl.BlockSpec((1, LANES), lambda bi, h, i: (0, 0))],
        out_specs=pl.BlockSpec((1, tq, LANES), lambda bi, h, i: (bi, i, h)),
        scratch_shapes=[pltpu.VMEM((2, 1, tq), jnp.float32),
                        pltpu.VMEM((2, 1, tq), jnp.float32),
                        pltpu.VMEM((2, LANES, tq), jnp.float32)],
        compiler_params=pltpu.CompilerParams(vmem_limit_bytes=VMEM_LIMIT),
        name="diff_attn",
    )(qd, kd4, kd_c, vdt, vdt_c, *lams, subln)


def _na_bias_table(rpb):
    heads = rpb.shape[0]
    d = np.arange(NA_ROWS)[:, None]
    i = np.arange(NA_ROWS)[None, :]
    row_idx = i - d + (NA_ROWS - 1)
    wq = np.arange(GRID_W)[:, None]
    wk = np.arange(GRID_W)[None, :]
    start = np.clip(wq - NA_COLS // 2, 0, GRID_W - NA_COLS)
    inside = (wk >= start) & (wk < start + NA_COLS)
    col_idx = np.clip(wk - wq + (NA_COLS - 1), 0, 2 * NA_COLS - 2)
    t = rpb[:, row_idx]
    t = t[:, :, :, col_idx]
    t = jnp.where(inside[None, None, None], t * LOG2E, NEG)
    t = t.transpose(1, 0, 3, 2, 4)
    return t.reshape(NA_ROWS, heads, GRID_W, NA_ROWS * GRID_W).astype(jnp.float32)


def _na_attn_kernel(q_ref, k_ref, v_ref, kc_ref, vc_ref, bias_ref, gain_ref, o_ref, *, rows_n, rows_per_step):
    step = pl.program_id(2)
    win = NA_ROWS * GRID_W
    lane = lax.broadcasted_iota(jnp.int32, (GRID_W, LANES), 1)
    low = lane < HEAD_DIM
    kc = kc_ref[0]
    vc = vc_ref[0]
    gain = gain_ref[...]

    def row(j, carry):
        r = step * rows_per_step + j
        r0 = jnp.clip(r - NA_ROWS // 2, 0, rows_n - NA_ROWS)
        d = r - r0
        start = pl.multiple_of(r0 * GRID_W, GRID_W)
        qrow = q_ref[0, pl.ds(pl.multiple_of(j * GRID_W, GRID_W), GRID_W), :]
        kw = k_ref[0, pl.ds(start, win), :]
        vw = v_ref[0, pl.ds(start, win), :]
        zero = jnp.zeros_like(qrow)
        heads_out = []
        for hh in range(2):
            qz = jnp.where(low if hh == 0 else ~low, qrow, zero)
            s_lat = lax.dot_general(qz, kw, _NT, preferred_element_type=jnp.float32) + bias_ref[d, hh]
            s_ctx = lax.dot_general(qz, kc, _NT, preferred_element_type=jnp.float32)
            m = jnp.maximum(jnp.max(s_lat, axis=-1, keepdims=True), jnp.max(s_ctx, axis=-1, keepdims=True))
            p_lat = jnp.exp2(s_lat - m)
            p_ctx = jnp.exp2(s_ctx - m)
            l = jnp.sum(p_lat, axis=-1, keepdims=True) + jnp.sum(p_ctx, axis=-1, keepdims=True)
            o = (jnp.dot(p_lat.astype(jnp.bfloat16), vw, preferred_element_type=jnp.float32)
                 + jnp.dot(p_ctx.astype(jnp.bfloat16), vc, preferred_element_type=jnp.float32))
            heads_out.append(o / l)
        o = jnp.where(low, heads_out[0], heads_out[1])
        sq = o * o
        ms0 = jnp.sum(jnp.where(low, sq, 0.0), axis=-1, keepdims=True) * (1.0 / HEAD_DIM)
        ms1 = jnp.sum(jnp.where(low, 0.0, sq), axis=-1, keepdims=True) * (1.0 / HEAD_DIM)
        inv = jnp.where(low, lax.rsqrt(ms0 + NORM_EPS), lax.rsqrt(ms1 + NORM_EPS))
        o_ref[0, pl.ds(pl.multiple_of(j * GRID_W, GRID_W), GRID_W), :] = (o * inv * gain).astype(o_ref.dtype)
        return carry

    lax.fori_loop(0, rows_per_step, row, 0)


def _na_attn(qn, kn, vn, kn_c, vn_c, bias, out_gain):
    b, s, width = qn.shape
    pairs = width // LANES
    rows_n = s // GRID_W
    rps = NA_ROWS_PER_STEP
    c = kn_c.shape[1]
    tq = rps * GRID_W
    kern = functools.partial(_na_attn_kernel, rows_n=rows_n, rows_per_step=rps)
    whole = lambda bi, h, i: (bi, 0, h)
    return pl.pallas_call(
        kern,
        out_shape=jax.ShapeDtypeStruct((b, s, width), jnp.bfloat16),
        grid=(b, pairs, rows_n // rps),
        in_specs=[pl.BlockSpec((1, tq, LANES), lambda bi, h, i: (bi, i, h)),
                  pl.BlockSpec((1, s, LANES), whole),
                  pl.BlockSpec((1, s, LANES), whole),
                  pl.BlockSpec((1, c, LANES), whole),
                  pl.BlockSpec((1, c, LANES), whole),
                  pl.BlockSpec((NA_ROWS, 2, GRID_W, NA_ROWS * GRID_W), lambda bi, h, i: (0, h, 0, 0)),
                  pl.BlockSpec((1, LANES), lambda bi, h, i: (0, h))],
        out_specs=pl.BlockSpec((1, tq, LANES), lambda bi, h, i: (bi, i, h)),
        compiler_params=pltpu.CompilerParams(vmem_limit_bytes=VMEM_LIMIT),
        name="na_attn",
    )(qn, kn, vn, kn_c, vn_c, bias, out_gain)


def _out_route_kernel(od_ref, on_ref, x_ref, w_ref, g1_ref, gf_ref, sh_ref, sc_ref, wr_ref, br_ref,
                      x1_ref, hp_ref, route_ref, gate_ref, cnt_ref, carry_ref):
    first = (pl.program_id(0) == 0) & (pl.program_id(1) == 0)

    @pl.when(first)
    def _():
        carry_ref[...] = jnp.zeros(carry_ref.shape, jnp.float32)

    half = od_ref.shape[2]
    attn = (jnp.dot(od_ref[0], w_ref[:half], preferred_element_type=jnp.float32)
            + jnp.dot(on_ref[0], w_ref[half:], preferred_element_type=jnp.float32))
    x1 = x_ref[0] + g1_ref[0] * attn
    x1_ref[0] = x1
    y = x1 * lax.rsqrt(jnp.mean(x1 * x1, axis=-1, keepdims=True) + NORM_EPS) * gf_ref[...]
    h2 = y * (1.0 + sc_ref[0]) + sh_ref[0]

    hb = lax.bitcast_convert_type(h2.astype(jnp.bfloat16).astype(jnp.float32), jnp.uint32)
    dh = hb.shape[1] // 2
    hp_ref[0] = (hb[:, :dh] >> 16) | (hb[:, dh:] & jnp.uint32(0xFFFF0000))

    logits = jnp.dot(h2, wr_ref[...], precision=lax.Precision.HIGHEST,
                     preferred_element_type=jnp.float32) + br_ref[...]
    tm = logits.shape[0]
    lane = lax.broadcasted_iota(jnp.int32, logits.shape, 1)
    lane_f = lane.astype(jnp.float32)
    vals, idxs = [], []
    cur = logits
    for _ in range(TOP_K):
        mx = jnp.max(cur, axis=-1, keepdims=True)
        ik = jnp.min(jnp.where(cur == mx, lane_f, float(LANES)), axis=-1, keepdims=True)
        vals.append(mx)
        idxs.append(ik)
        cur = jnp.where(lane_f == ik, -3e38, cur)
    exps = [jnp.exp(v - vals[0]) for v in vals]
    denom = exps[0] + exps[1] + exps[2] + exps[3]

    onehot = jnp.zeros(logits.shape, jnp.float32)
    for ik in idxs:
        onehot = onehot + jnp.where(lane_f == ik, 1.0, 0.0)
    ti = lax.broadcasted_iota(jnp.int32, (tm, tm), 0)
    tj = lax.broadcasted_iota(jnp.int32, (tm, tm), 1)
    lower = jnp.where(tj < ti, 1.0, 0.0).astype(jnp.bfloat16)
    prefix = jnp.dot(lower, onehot.astype(jnp.bfloat16), preferred_element_type=jnp.float32) + carry_ref[...]
    route = jnp.zeros(logits.shape, jnp.float32)
    gates = jnp.zeros(logits.shape, jnp.float32)
    for k in range(TOP_K):
        rank = jnp.sum(jnp.where(lane_f == idxs[k], prefix, 0.0), axis=-1, keepdims=True)
        route = jnp.where(lane == k, idxs[k], route)
        route = jnp.where(lane == TOP_K + k, rank, route)
        gates = jnp.where(lane == k, exps[k] / denom, gates)
    route_ref[0] = route.astype(jnp.int32)
    gate_ref[0] = gates
    carry_ref[...] = carry_ref[...] + jnp.sum(onehot, axis=0, keepdims=True)
    cnt_ref[...] = carry_ref[...]


def _out_route(o_diff, o_na, x, w_out, g1, g_ffn, sh2, sc2, w_router, b_router):
    b, s, d = x.shape
    tm = ROUTE_TM
    half = o_diff.shape[2]
    e = w_router.shape[1]
    wr = jnp.zeros((d, LANES), jnp.float32).at[:, :e].set(w_router)
    br = jnp.full((1, LANES), NEG, jnp.float32).at[0, :e].set(b_router)
    tile = lambda bi, i: (bi, i, 0)
    row = lambda bi, i: (0, 0)
    per_b = lambda bi, i: (bi, 0, 0)
    return pl.pallas_call(
        _out_route_kernel,
        out_shape=[jax.ShapeDtypeStruct((b, s, d), jnp.float32),
                   jax.ShapeDtypeStruct((b, s, d // 2), jnp.uint32),
                   jax.ShapeDtypeStruct((b, s, LANES), jnp.int32),
                   jax.ShapeDtypeStruct((b, s, LANES), jnp.float32),
                   jax.ShapeDtypeStruct((1, LANES), jnp.float32)],
        grid=(b, s // tm),
        in_specs=[pl.BlockSpec((1, tm, half), tile),
                  pl.BlockSpec((1, tm, half), tile),
                  pl.BlockSpec((1, tm, d), tile),
                  pl.BlockSpec(w_out.shape, row),
                  pl.BlockSpec((1, 1, d), per_b),
                  pl.BlockSpec((1, d), row),
                  pl.BlockSpec((1, 1, d), per_b),
                  pl.BlockSpec((1, 1, d), per_b),
                  pl.BlockSpec((d, LANES), row),
                  pl.BlockSpec((1, LANES), row)],
        out_specs=[pl.BlockSpec((1, tm, d), tile),
                   pl.BlockSpec((1, tm, d // 2), tile),
                   pl.BlockSpec((1, tm, LANES), tile),
                   pl.BlockSpec((1, tm, LANES), tile),
                   pl.BlockSpec((1, LANES), row)],
        scratch_shapes=[pltpu.VMEM((1, LANES), jnp.float32)],
        compiler_params=pltpu.CompilerParams(
            dimension_semantics=("arbitrary", "arbitrary"), vmem_limit_bytes=VMEM_LIMIT),
        name="out_route",
    )(o_diff, o_na, x, w_out, g1, g_ffn, sh2, sc2, wr, br)


def _row_copy(src_ref, src_row, dst_ref, dst_row, sem):
    return pltpu.make_async_copy(src_ref.at[pl.ds(src_row, 1), :], dst_ref.at[pl.ds(dst_row, 1), :], sem)


def _dispatch_kernel(dest_ref, h_ref, xs_in_ref, xs_ref, sem):
    del xs_in_ref
    n = dest_ref.shape[0]

    def start(j, carry):
        _row_copy(h_ref, j // TOP_K, xs_ref, dest_ref[j], sem).start()
        return carry

    def wait(j, carry):
        _row_copy(h_ref, j // TOP_K, xs_ref, dest_ref[j], sem).wait()
        return carry

    lax.fori_loop(0, n, start, 0)
    lax.fori_loop(0, n, wait, 0)


def _dispatch(dest_flat, hp, n_rows):
    t, dw = hp.shape
    tm = MOVE_TM
    xs0 = jnp.zeros((n_rows, dw), hp.dtype)
    return pl.pallas_call(
        _dispatch_kernel,
        out_shape=jax.ShapeDtypeStruct((n_rows, dw), hp.dtype),
        grid=(t // tm,),
        in_specs=[pl.BlockSpec((tm * TOP_K,), lambda i: (i,), memory_space=pltpu.SMEM),
                  pl.BlockSpec((tm, dw), lambda i: (i, 0)),
                  pl.BlockSpec(memory_space=pl.ANY)],
        out_specs=pl.BlockSpec(memory_space=pl.ANY),
        scratch_shapes=[pltpu.SemaphoreType.DMA(())],
        input_output_aliases={2: 0},
        compiler_params=pltpu.CompilerParams(vmem_limit_bytes=VMEM_LIMIT),
        name="dispatch",
    )(dest_flat, hp, xs0)


def _experts_kernel(be_ref, nv_ref, xs_ref, wgu_ref, bgu_ref, wdn_ref, bdn_ref, ys_ref):
    i = pl.program_id(0)

    @pl.when(i < nv_ref[0])
    def _():
        u = xs_ref[...]
        dh = u.shape[1]
        lo = lax.bitcast_convert_type(u << 16, jnp.float32).astype(jnp.bfloat16)
        hi = lax.bitcast_convert_type(u & jnp.uint32(0xFFFF0000), jnp.float32).astype(jnp.bfloat16)
        gu = (jnp.dot(lo, wgu_ref[0, :dh], preferred_element_type=jnp.float32)
              + jnp.dot(hi, wgu_ref[0, dh:], preferred_element_type=jnp.float32) + bgu_ref[0])
        f = gu.shape[1] // 2
        glu = jnp.minimum(gu[:, :f], SWIGLU_LIMIT)
        lin = jnp.clip(gu[:, f:], -SWIGLU_LIMIT, SWIGLU_LIMIT)
        act = glu * (1.0 / (1.0 + jnp.exp(-SWIGLU_ALPHA * glu))) * (lin + 1.0)
        ys_ref[...] = jnp.dot(act.astype(jnp.bfloat16), wdn_ref[0],
                              preferred_element_type=jnp.float32) + bdn_ref[0]

    @pl.when(i >= nv_ref[0])
    def _():
        ys_ref[...] = jnp.zeros(ys_ref.shape, ys_ref.dtype)


def _experts(block_expert, n_valid, xs, w_gu, b_gu, w_dn, b_dn):
    n_rows, dw = xs.shape
    tm = EXPERT_TM
    e, d, f2 = w_gu.shape
    grid_spec = pltpu.PrefetchScalarGridSpec(
        num_scalar_prefetch=2,
        grid=(n_rows // tm,),
        in_specs=[pl.BlockSpec((tm, dw), lambda i, be, nv: (i, 0)),
                  pl.BlockSpec((1, d, f2), lambda i, be, nv: (be[i], 0, 0)),
                  pl.BlockSpec((1, 1, f2), lambda i, be, nv: (be[i], 0, 0)),
                  pl.BlockSpec((1, f2 // 2, d), lambda i, be, nv: (be[i], 0, 0)),
                  pl.BlockSpec((1, 1, d), lambda i, be, nv: (be[i], 0, 0))],
        out_specs=pl.BlockSpec((tm, d), lambda i, be, nv: (i, 0)),
    )
    return pl.pallas_call(
        _experts_kernel,
        out_shape=jax.ShapeDtypeStruct((n_rows, d), jnp.float32),
        grid_spec=grid_spec,
        compiler_params=pltpu.CompilerParams(vmem_limit_bytes=VMEM_LIMIT),
        name="experts",
    )(block_expert, n_valid, xs, w_gu, b_gu.reshape(e, 1, f2), w_dn, b_dn.reshape(e, 1, d))


def _combine_kernel(dest_ref, gate_ref, x1_ref, g2_ref, ys_ref, o_ref, buf_ref, sem):
    n = dest_ref.shape[0]

    def start(j, carry):
        _row_copy(ys_ref, dest_ref[j], buf_ref.at[j % TOP_K], j // TOP_K, sem).start()
        return carry

    def wait(j, carry):
        _row_copy(ys_ref, dest_ref[j], buf_ref.at[j % TOP_K], j // TOP_K, sem).wait()
        return carry

    lax.fori_loop(0, n, start, 0)
    lax.fori_loop(0, n, wait, 0)
    gates = gate_ref[...]
    moe = gates[:, 0:1] * buf_ref[0]
    for k in range(1, TOP_K):
        moe = moe + gates[:, k:k + 1] * buf_ref[k]
    o_ref[...] = x1_ref[...] + g2_ref[0] * moe


def _combine(dest_flat, gates, x1, g2, ys, tiles_per_batch):
    t, d = x1.shape
    tm = MOVE_TM
    return pl.pallas_call(
        _combine_kernel,
        out_shape=jax.ShapeDtypeStruct((t, d), jnp.float32),
        grid=(t // tm,),
        in_specs=[pl.BlockSpec((tm * TOP_K,), lambda i: (i,), memory_space=pltpu.SMEM),
                  pl.BlockSpec((tm, LANES), lambda i: (i, 0)),
                  pl.BlockSpec((tm, d), lambda i: (i, 0)),
                  pl.BlockSpec((1, 1, d), lambda i: (i // tiles_per_batch, 0, 0)),
                  pl.BlockSpec(memory_space=pl.ANY)],
        out_specs=pl.BlockSpec((tm, d), lambda i: (i, 0)),
        scratch_shapes=[pltpu.VMEM((TOP_K, tm, d), jnp.float32), pltpu.SemaphoreType.DMA(())],
        compiler_params=pltpu.CompilerParams(vmem_limit_bytes=VMEM_LIMIT),
        name="combine",
    )(dest_flat, gates, x1, g2, ys)


def _rope_tables(s):
    pos = jnp.arange(s, dtype=jnp.int32)
    inv = 1.0 / (ROPE_BASE ** (jnp.arange(ROPE_FREQS, dtype=jnp.float32) / ROPE_FREQS))
    ang_r = (pos // GRID_W).astype(jnp.float32)[:, None] * inv
    ang_c = (pos % GRID_W).astype(jnp.float32)[:, None] * inv
    ang = jnp.concatenate([ang_r, ang_r, ang_c, ang_c], axis=-1)
    sign = jnp.asarray(np.tile(np.repeat([-1.0, 1.0], ROPE_FREQS), 2), jnp.float32)
    cos_t = jnp.tile(jnp.cos(ang), (1, LANES // HEAD_DIM))
    sin_t = jnp.tile(jnp.sin(ang) * sign, (1, LANES // HEAD_DIM))
    return cos_t, sin_t


def kernel(x, c, ctx, c_ctx, w_ada, b_ada, g_attn, w_in, q_norm_diff, k_norm_diff, lam_q1, lam_k1, lam_q2,
           lam_k2, subln_diff, q_norm_na, k_norm_na, rpb_na, out_norm_na, w_out, g_ffn, w_router, b_router,
           w_gate_up, b_gate_up, w_down, b_down):
    depth = w_ada.shape[0]
    assert depth == 1, "single-layer kernel"
    b, s, d = x.shape
    n_ctx = ctx.shape[1]
    rows_n = s // GRID_W
    assert rows_n >= NA_ROWS and s % PROJ_TM == 0 and rows_n % NA_ROWS_PER_STEP == 0
    lam_init = 0.8 - 0.6 * math.exp(-0.3 * 0)

    rows = -(-(b + 1) // 8) * 8
    cv = jnp.zeros((rows, d), jnp.float32).at[:b].set(c).at[b].set(c_ctx)
    mod = _adaln(cv, w_ada[0], b_ada[0]).reshape(rows, 6, d)
    lat = [mod[:b, i][:, None, :] for i in range(6)]
    cxm = [jnp.broadcast_to(mod[b, i][None, None, :], (b, 1, d)) for i in range(6)]
    sh1, sc1, g1, sh2, sc2, g2 = lat

    w_in_b = w_in[0].astype(jnp.bfloat16)
    gidx = np.arange(256) // HEAD_DIM
    gsum = jnp.asarray(gidx[:, None] == gidx[None, :], jnp.bfloat16)
    tile4 = lambda v, reps: jnp.tile(v.reshape(1, -1), (1, reps))
    norms = (tile4(q_norm_diff[0], 8), tile4(k_norm_diff[0], 8), tile4(q_norm_na[0], 8), tile4(k_norm_na[0], 8))
    cos_t, sin_t = _rope_tables(s)
    g_attn2 = g_attn[0].reshape(1, d)

    qd, kd, vdt, qn, kn, vn = _proj(x, g_attn2, sh1, sc1, w_in_b, gsum, cos_t, sin_t, norms,
                                    groups=("qd", "kd", "vd", "qn", "kn", "vn"), rope=True, tm=PROJ_TM)
    kd_c, vdt_c, kn_c, vn_c = _proj(ctx, g_attn2, cxm[0], cxm[1], w_in_b, gsum, cos_t[:n_ctx], sin_t[:n_ctx],
                                    norms, groups=("kd", "vd", "kn", "vn"), rope=False, tm=n_ctx)

    lams = tuple(v[0].reshape(1, HEAD_DIM) for v in (lam_q1, lam_k1, lam_q2, lam_k2))
    o_diff = _diff_attn(qd, kd, kd_c, vdt, vdt_c, lams, subln_diff[0].reshape(1, LANES), lam_init=lam_init)
    o_na = _na_attn(qn, kn, vn, kn_c, vn_c, _na_bias_table(rpb_na[0]), out_norm_na[0].reshape(1, -1))

    x1, hp, route, gates, counts = _out_route(o_diff, o_na, x, w_out[0].astype(jnp.bfloat16), g1,
                                              g_ffn[0].reshape(1, d), sh2, sc2, w_router[0], b_router[0])

    t = b * s
    n_exp = w_router.shape[2]
    cnt = counts[0, :n_exp].astype(jnp.int32)
    padded = (cnt + EXPERT_TM - 1) // EXPERT_TM * EXPERT_TM
    pad_ends = jnp.cumsum(padded)
    pad_starts = pad_ends - padded
    route = route.reshape(t, LANES)
    dest = (pad_starts[route[:, :TOP_K]] + route[:, TOP_K:2 * TOP_K]).reshape(-1)
    n_tiles = (t * TOP_K) // EXPERT_TM + n_exp
    block_expert = jnp.minimum(
        jnp.searchsorted(pad_ends, jnp.arange(n_tiles, dtype=jnp.int32) * EXPERT_TM, side="right"),
        n_exp - 1).astype(jnp.int32)
    n_valid = (pad_ends[-1:] // EXPERT_TM).astype(jnp.int32)

    xs = _dispatch(dest, hp.reshape(t, d // 2), n_tiles * EXPERT_TM)
    ys = _experts(block_expert, n_valid, xs, w_gate_up[0].astype(jnp.bfloat16), b_gate_up[0],
                  w_down[0].astype(jnp.bfloat16), b_down[0])
    out = _combine(dest, gates.reshape(t, LANES), x1.reshape(t, d), g2, ys, s // MOVE_TM)
    return out.reshape(b, s, d)
```

```python
import functools
import math

import jax
import jax.numpy as jnp
import numpy as np
from jax import lax
from jax.experimental import pallas as pl
from jax.experimental.pallas import tpu as pltpu

GRID_W = 64
HEAD_DIM = 64
NA_ROWS = 8
NA_COLS = 16
ROPE_BASE = 10000.0
ROPE_FREQS = HEAD_DIM // 4
N_EXPERTS = 32
TOP_K = 4
SWIGLU_ALPHA = 1.702
SWIGLU_LIMIT = 7.0
NORM_EPS = 1e-6

LANES = 128
NEG = -1e30
LOG2E = math.log2(math.e)
Q_SCALE = HEAD_DIM ** -0.5 * LOG2E
EXP2_SAFE_SCORE = 50.0

PROJ_TM = 512
DIFF_TQ = 256
DIFF_SCORE_BUFS = 4
NA_ROWS_PER_STEP = 16
NA_ROW_UNROLL = 8
ROUTE_TM = 512
EXPERT_TM = 512
RUN_ALIGN = 16
TILE_SLOTS = ROUTE_TM * TOP_K + N_EXPERTS * RUN_ALIGN
VMEM_LIMIT = 48 * 1024 * 1024
EXPERTS_VMEM_LIMIT = 56 * 1024 * 1024

_NT = (((1,), (1,)), ((), ()))


def _f32(x):
    return x.astype(jnp.float32)


def _adaln_kernel(cv_ref, w_ref, b_ref, o_ref):
    cv = cv_ref[...]
    act = cv * (1.0 / (1.0 + jnp.exp(-cv)))
    o_ref[...] = jnp.dot(act, w_ref[...], precision=lax.Precision.HIGHEST,
                         preferred_element_type=jnp.float32) + b_ref[...]


def _adaln(cv, w_ada, b_ada):
    rows, d = cv.shape
    n = w_ada.shape[1]
    tn = 1024
    return pl.pallas_call(
        _adaln_kernel,
        out_shape=jax.ShapeDtypeStruct((rows, n), jnp.float32),
        grid=(n // tn,),
        in_specs=[pl.BlockSpec((rows, d), lambda j: (0, 0)),
                  pl.BlockSpec((d, tn), lambda j: (0, j)),
                  pl.BlockSpec((1, tn), lambda j: (0, j))],
        out_specs=pl.BlockSpec((rows, tn), lambda j: (0, j)),
        compiler_params=pltpu.CompilerParams(vmem_limit_bytes=VMEM_LIMIT),
        name="adaln",
    )(cv, w_ada, b_ada.reshape(1, n))


def _proj_kernel(*refs, groups, rope, d_model):
    (x_ref, g_ref, sh_ref, sc_ref, w_ref, gsum_ref, cos_ref, sin_ref,
     qnd_ref, knd_ref, qnn_ref, knn_ref) = refs[:12]
    outs = dict(zip(groups, refs[12:]))
    width = 4 * LANES

    x = x_ref[0]
    y = x * lax.rsqrt(jnp.mean(x * x, axis=-1, keepdims=True) + NORM_EPS) * g_ref[...]
    h = (y * (1.0 + sc_ref[0]) + sh_ref[0]).astype(jnp.bfloat16)

    group_col = {"qd": 0, "kd": 1, "vd": 2, "qn": 3, "kn": 4, "vn": 5}
    gains = {"qd": qnd_ref[...] * Q_SCALE, "kd": knd_ref[...],
             "qn": qnn_ref[...] * Q_SCALE, "kn": knn_ref[...]}
    lane = lax.broadcasted_iota(jnp.int32, (x.shape[0], LANES), 1)
    first_half = (lane % (2 * ROPE_FREQS)) < ROPE_FREQS

    for name in groups:
        c0 = group_col[name] * width
        p = jnp.dot(h, w_ref[:, c0:c0 + width], preferred_element_type=jnp.float32)
        if name in gains:
            blocks = []
            for j in range(width // 256):
                blk = p[:, j * 256:(j + 1) * 256]
                ss = jnp.dot((blk * blk).astype(jnp.bfloat16), gsum_ref[...],
                             preferred_element_type=jnp.float32)
                blocks.append(blk * lax.rsqrt(ss * (1.0 / HEAD_DIM) + NORM_EPS))
            p = jnp.concatenate(blocks, axis=1) * gains[name]
        if rope and name in ("qd", "kd"):
            blocks = []
            for j in range(width // LANES):
                blk = p[:, j * LANES:(j + 1) * LANES]
                partner = jnp.where(first_half,
                                    pltpu.roll(blk, LANES - ROPE_FREQS, axis=1),
                                    pltpu.roll(blk, ROPE_FREQS, axis=1))
                blocks.append(blk * cos_ref[...] + partner * sin_ref[...])
            p = jnp.concatenate(blocks, axis=1)
        if name == "vd":
            pt = p.T.reshape(width // LANES, LANES, p.shape[0])
            outs[name][0, :, 0] = pt.astype(jnp.bfloat16)
        else:
            outs[name][0] = p.astype(jnp.bfloat16)


def _proj(x, g_attn, shift, scale, w_in, gsum, cos_t, sin_t, norms, *, groups, rope, tm):
    b, s, d = x.shape
    nt = s // tm
    width = 4 * LANES
    out_shape, out_specs = [], []
    for name in groups:
        if name == "vd":
            out_shape.append(jax.ShapeDtypeStruct((b, 4, nt, LANES, tm), jnp.bfloat16))
            out_specs.append(pl.BlockSpec((1, 4, 1, LANES, tm), lambda bi, i: (bi, 0, i, 0, 0)))
        else:
            out_shape.append(jax.ShapeDtypeStruct((b, s, width), jnp.bfloat16))
            out_specs.append(pl.BlockSpec((1, tm, width), lambda bi, i: (bi, i, 0)))
    row = lambda bi, i: (0, 0)
    per_b = lambda bi, i: (bi, 0, 0)
    kern = functools.partial(_proj_kernel, groups=groups, rope=rope, d_model=d)
    return pl.pallas_call(
        kern,
        out_shape=out_shape,
        grid=(b, nt),
        in_specs=[pl.BlockSpec((1, tm, d), lambda bi, i: (bi, i, 0)),
                  pl.BlockSpec((1, d), row),
                  pl.BlockSpec((1, 1, d), per_b),
                  pl.BlockSpec((1, 1, d), per_b),
                  pl.BlockSpec(w_in.shape, row),
                  pl.BlockSpec(gsum.shape, row),
                  pl.BlockSpec((tm, LANES), lambda bi, i: (i, 0)),
                  pl.BlockSpec((tm, LANES), lambda bi, i: (i, 0)),
                  pl.BlockSpec((1, width), row), pl.BlockSpec((1, width), row),
                  pl.BlockSpec((1, width), row), pl.BlockSpec((1, width), row)],
        out_specs=out_specs,
        compiler_params=pltpu.CompilerParams(vmem_limit_bytes=VMEM_LIMIT),
        name="proj_rope" if rope else "proj_ctx",
    )(x, g_attn, shift, scale, w_in, gsum, cos_t, sin_t, *norms)


def _diff_attn_kernel(q_ref, k_ref, kc_ref, vt_ref, vtc_ref, lq1_ref, lk1_ref, lq2_ref, lk2_ref,
                      subln_ref, o_ref, m_ref, l_ref, acc_ref, *s_refs, lam_init, n_kblocks, stabilise):
    q = q_ref[0]
    lane = lax.broadcasted_iota(jnp.int32, q.shape, 1)
    zero = jnp.zeros_like(q)
    qz = (jnp.where(lane < HEAD_DIM, q, zero), jnp.where(lane >= HEAD_DIM, q, zero))

    l_ref[...] = jnp.zeros(l_ref.shape, jnp.float32)
    acc_ref[...] = jnp.zeros(acc_ref.shape, jnp.float32)

    def sum8(p):
        return jnp.sum(p.reshape(p.shape[0] // 8, 8, p.shape[1]), axis=0)

    def scores(k, mp):
        return lax.dot_general(k, qz[mp], _NT, preferred_element_type=jnp.float32)

    if stabilise:
        m_ref[...] = jnp.full(m_ref.shape, NEG, jnp.float32)

        def block(k, vt):
            for mp in range(2):
                s = scores(k, mp)
                m_old = m_ref[mp]
                m_new = jnp.maximum(m_old, jnp.max(s, axis=0, keepdims=True))
                alpha = jnp.exp2(m_old - m_new)
                p = jnp.exp2(s - m_new)
                l_ref[mp] = alpha * l_ref[mp] + sum8(p)
                acc_ref[mp] = alpha * acc_ref[mp] + jnp.dot(vt, p.astype(jnp.bfloat16),
                                                            preferred_element_type=jnp.float32)
                m_ref[mp] = m_new

        def body(i, carry):
            block(k_ref[0, i], vt_ref[0, 0, i])
            return carry

        lax.fori_loop(0, n_kblocks, body, 0)
        block(kc_ref[0], vtc_ref[0, 0, 0])
    else:
        n_bufs = len(s_refs)
        ahead = n_bufs // 2

        def produce_into(k, s_ref):
            for mp in range(2):
                s_ref[mp, :k.shape[0]] = scores(k, mp)

        def consume(s_ref, vt):
            for mp in range(2):
                p = jnp.exp2(s_ref[mp, :vt.shape[1]])
                l_ref[mp] += sum8(p)
                acc_ref[mp] += jnp.dot(vt, p.astype(jnp.bfloat16), preferred_element_type=jnp.float32)

        for i in range(ahead):
            produce_into(k_ref[0, i], s_refs[i])

        def body(j, carry):
            base = n_bufs * j
            for u in range(n_bufs):
                produce_into(k_ref[0, base + u + ahead], s_refs[(u + ahead) % n_bufs])
                consume(s_refs[u], vt_ref[0, 0, base + u])
            return carry

        n_main = (n_kblocks - ahead) // n_bufs
        lax.fori_loop(0, n_main, body, 0)
        for i in range(n_main * n_bufs, n_kblocks + 1):
            nxt = i + ahead
            if nxt <= n_kblocks:
                produce_into(kc_ref[0] if nxt == n_kblocks else k_ref[0, nxt], s_refs[nxt % n_bufs])
            consume(s_refs[i % n_bufs], vtc_ref[0, 0, 0] if i == n_kblocks else vt_ref[0, 0, i])

    lam = (jnp.exp(jnp.sum(lq1_ref[...] * lk1_ref[...], keepdims=True))
           - jnp.exp(jnp.sum(lq2_ref[...] * lk2_ref[...], keepdims=True)) + lam_init)
    l1 = jnp.sum(l_ref[0], axis=0, keepdims=True)
    l2 = jnp.sum(l_ref[1], axis=0, keepdims=True)
    o = acc_ref[0] / l1 - lam * (acc_ref[1] / l2)
    ot = o.T
    ot = ot * lax.rsqrt(jnp.mean(ot * ot, axis=-1, keepdims=True) + NORM_EPS)
    o_ref[0] = (ot * subln_ref[...] * (1.0 - lam_init)).astype(o_ref.dtype)


def _diff_attn(qd, kd, kd_c, vdt, vdt_c, lams, subln, *, lam_init, stabilise):
    b, s, width = qd.shape
    heads = width // LANES
    tk = vdt.shape[-1]
    nkb = s // tk
    c = kd_c.shape[1]
    kd4 = kd.reshape(b, nkb, tk, width)
    tq = DIFF_TQ
    assert nkb >= DIFF_SCORE_BUFS // 2 and c <= tk
    kern = functools.partial(_diff_attn_kernel, lam_init=lam_init, n_kblocks=nkb, stabilise=stabilise)
    vec = pl.BlockSpec((1, HEAD_DIM), lambda bi, h, i: (0, 0))
    return pl.pallas_call(
        kern,
        out_shape=jax.ShapeDtypeStruct((b, s, width), jnp.bfloat16),
        grid=(b, heads, s // tq),
        in_specs=[pl.BlockSpec((1, tq, LANES), lambda bi, h, i: (bi, i, h)),
                  pl.BlockSpec((1, nkb, tk, LANES), lambda bi, h, i: (bi, 0, 0, h)),
                  pl.BlockSpec((1, c, LANES), lambda bi, h, i: (bi, 0, h)),
                  pl.BlockSpec((1, 1, nkb, LANES, tk), lambda bi, h, i: (bi, h, 0, 0, 0)),
                  pl.BlockSpec((1, 1, 1, LANES, c), lambda bi, h, i: (bi, h, 0, 0, 0)),
                  vec, vec, vec, vec,
                  pl.BlockSpec((1, LANES), lambda bi, h, i: (0, 0))],
        out_specs=pl.BlockSpec((1, tq, LANES), lambda bi, h, i: (bi, i, h)),
        scratch_shapes=[pltpu.VMEM((2, 1, tq), jnp.float32),
                        pltpu.VMEM((2, 8, tq), jnp.float32),
                        pltpu.VMEM((2, LANES, tq), jnp.float32)]
                       + [pltpu.VMEM((2, tk, tq), jnp.float32)] * DIFF_SCORE_BUFS,
        compiler_params=pltpu.CompilerParams(vmem_limit_bytes=VMEM_LIMIT),
        name="diff_attn",
    )(qd, kd4, kd_c, vdt, vdt_c, *lams, subln)


def _na_bias_table(rpb):
    heads = rpb.shape[0]
    d = np.arange(NA_ROWS)[:, None]
    i = np.arange(NA_ROWS)[None, :]
    row_idx = i - d + (NA_ROWS - 1)
    wq = np.arange(GRID_W)[:, None]
    wk = np.arange(GRID_W)[None, :]
    start = np.clip(wq - NA_COLS // 2, 0, GRID_W - NA_COLS)
    inside = (wk >= start) & (wk < start + NA_COLS)
    col_idx = np.clip(wk - wq + (NA_COLS - 1), 0, 2 * NA_COLS - 2)
    t = rpb[:, row_idx]
    t = t[:, :, :, col_idx]
    t = jnp.where(inside[None, None, None], t * LOG2E, NEG)
    t = t.transpose(1, 0, 3, 2, 4)
    return t.reshape(NA_ROWS, heads // 2, 2 * GRID_W, NA_ROWS * GRID_W).astype(jnp.float32)


def _na_attn_kernel(q_ref, k_ref, v_ref, kc_ref, vc_ref, bias_ref, gain_ref, o_ref, octx_ref, lctx_ref, *,
                    rows_n, rows_per_step, stabilise):
    step = pl.program_id(2)
    win = NA_ROWS * GRID_W
    low = lax.broadcasted_iota(jnp.int32, (GRID_W, LANES), 1) < HEAD_DIM
    kc = kc_ref[0]
    vc = vc_ref[0]
    gain = gain_ref[...]

    def head_lanes(q, hh):
        keep = (lax.broadcasted_iota(jnp.int32, q.shape, 1) < HEAD_DIM) == (hh == 0)
        return jnp.where(keep, q, jnp.zeros_like(q))

    if not stabilise:
        q_all = q_ref[0]
        for hh in range(2):
            p = jnp.exp2(lax.dot_general(head_lanes(q_all, hh), kc, _NT, preferred_element_type=jnp.float32))
            lctx_ref[hh] = jnp.broadcast_to(jnp.sum(p, axis=-1, keepdims=True), lctx_ref.shape[1:])
            octx_ref[hh] = jnp.dot(p.astype(jnp.bfloat16), vc, preferred_element_type=jnp.float32)

    def row(j):
        r = step * rows_per_step + j
        r0 = jnp.clip(r - NA_ROWS // 2, 0, rows_n - NA_ROWS)
        d = r - r0
        start = pl.multiple_of(r0 * GRID_W, GRID_W)
        rows = pl.ds(pl.multiple_of(j * GRID_W, GRID_W), GRID_W)
        qrow = q_ref[0, rows, :]
        kw = k_ref[0, pl.ds(start, win), :]
        vw = v_ref[0, pl.ds(start, win), :]
        q2 = jnp.concatenate([head_lanes(qrow, 0), head_lanes(qrow, 1)], axis=0)
        s = lax.dot_general(q2, kw, _NT, preferred_element_type=jnp.float32) + bias_ref[d, 0]
        if stabilise:
            s_ctx = lax.dot_general(q2, kc, _NT, preferred_element_type=jnp.float32)
            m = jnp.maximum(jnp.max(s, axis=-1, keepdims=True), jnp.max(s_ctx, axis=-1, keepdims=True))
            p = jnp.exp2(s - m)
            p_ctx = jnp.exp2(s_ctx - m)
            l = jnp.sum(p, axis=-1, keepdims=True) + jnp.sum(p_ctx, axis=-1, keepdims=True)
            o2 = (jnp.dot(p.astype(jnp.bfloat16), vw, preferred_element_type=jnp.float32)
                  + jnp.dot(p_ctx.astype(jnp.bfloat16), vc, preferred_element_type=jnp.float32)) / l
            o = jnp.where(low, o2[:GRID_W], o2[GRID_W:])
        else:
            p = jnp.exp2(s)
            l = jnp.sum(p, axis=-1, keepdims=True)
            o2 = jnp.dot(p.astype(jnp.bfloat16), vw, preferred_element_type=jnp.float32)
            o = jnp.where(low,
                          (o2[:GRID_W] + octx_ref[0, rows, :]) / (l[:GRID_W] + lctx_ref[0, rows, :]),
                          (o2[GRID_W:] + octx_ref[1, rows, :]) / (l[GRID_W:] + lctx_ref[1, rows, :]))
        sq = o * o
        ms0 = jnp.sum(jnp.where(low, sq, 0.0), axis=-1, keepdims=True) * (1.0 / HEAD_DIM)
        ms1 = jnp.sum(jnp.where(low, 0.0, sq), axis=-1, keepdims=True) * (1.0 / HEAD_DIM)
        inv = jnp.where(low, lax.rsqrt(ms0 + NORM_EPS), lax.rsqrt(ms1 + NORM_EPS))
        o_ref[0, rows, :] = (o * inv * gain).astype(o_ref.dtype)

    def row_group(jj, carry):
        for u in range(NA_ROW_UNROLL):
            row(NA_ROW_UNROLL * jj + u)
        return carry

    lax.fori_loop(0, rows_per_step // NA_ROW_UNROLL, row_group, 0)


def _na_attn(qn, kn, vn, kn_c, vn_c, bias, out_gain, *, stabilise):
    b, s, width = qn.shape
    pairs = width // LANES
    rows_n = s // GRID_W
    rps = NA_ROWS_PER_STEP
    c = kn_c.shape[1]
    tq = rps * GRID_W
    kern = functools.partial(_na_attn_kernel, rows_n=rows_n, rows_per_step=rps, stabilise=stabilise)
    whole = lambda bi, h, i: (bi, 0, h)
    return pl.pallas_call(
        kern,
        out_shape=jax.ShapeDtypeStruct((b, s, width), jnp.bfloat16),
        grid=(b, pairs, rows_n // rps),
        in_specs=[pl.BlockSpec((1, tq, LANES), lambda bi, h, i: (bi, i, h)),
                  pl.BlockSpec((1, s, LANES), whole),
                  pl.BlockSpec((1, s, LANES), whole),
                  pl.BlockSpec((1, c, LANES), whole),
                  pl.BlockSpec((1, c, LANES), whole),
                  pl.BlockSpec((NA_ROWS, 1, 2 * GRID_W, NA_ROWS * GRID_W), lambda bi, h, i: (0, h, 0, 0)),
                  pl.BlockSpec((1, LANES), lambda bi, h, i: (0, h))],
        out_specs=pl.BlockSpec((1, tq, LANES), lambda bi, h, i: (bi, i, h)),
        scratch_shapes=[pltpu.VMEM((2, tq, LANES), jnp.float32), pltpu.VMEM((2, tq, LANES), jnp.float32)],
        compiler_params=pltpu.CompilerParams(vmem_limit_bytes=VMEM_LIMIT),
        name="na_attn",
    )(qn, kn, vn, kn_c, vn_c, bias, out_gain)


def _out_route_kernel(od_ref, on_ref, x_ref, w_ref, g1_ref, gf_ref, sh_ref, sc_ref, wr_ref, br_ref,
                      x1_ref, h2_ref, route_ref, gate_ref, cnt_ref):
    half = od_ref.shape[2]
    attn = (jnp.dot(od_ref[0], w_ref[:half], preferred_element_type=jnp.float32)
            + jnp.dot(on_ref[0], w_ref[half:], preferred_element_type=jnp.float32))
    x1 = x_ref[0] + g1_ref[0] * attn
    x1_ref[0] = x1
    y = x1 * lax.rsqrt(jnp.mean(x1 * x1, axis=-1, keepdims=True) + NORM_EPS) * gf_ref[...]
    h2 = y * (1.0 + sc_ref[0]) + sh_ref[0]
    h2_ref[0] = h2.astype(jnp.bfloat16)

    logits = jnp.dot(h2, wr_ref[...], precision=lax.Precision.HIGHEST,
                     preferred_element_type=jnp.float32) + br_ref[...]
    tm = logits.shape[0]
    lane = lax.broadcasted_iota(jnp.int32, logits.shape, 1)
    lane_f = lane.astype(jnp.float32)
    vals, idxs = [], []
    cur = logits
    for _ in range(TOP_K):
        mx = jnp.max(cur, axis=-1, keepdims=True)
        ik = jnp.min(jnp.where(cur == mx, lane_f, float(LANES)), axis=-1, keepdims=True)
        vals.append(mx)
        idxs.append(ik)
        cur = jnp.where(lane_f == ik, -3e38, cur)
    exps = [jnp.exp(v - vals[0]) for v in vals]
    denom = exps[0] + exps[1] + exps[2] + exps[3]

    onehot = jnp.zeros(logits.shape, jnp.float32)
    for ik in idxs:
        onehot = onehot + jnp.where(lane_f == ik, 1.0, 0.0)
    ti = lax.broadcasted_iota(jnp.int32, (tm, tm), 0)
    tj = lax.broadcasted_iota(jnp.int32, (tm, tm), 1)
    lower = jnp.where(tj < ti, 1.0, 0.0).astype(jnp.bfloat16)
    prefix = jnp.dot(lower, onehot.astype(jnp.bfloat16), preferred_element_type=jnp.float32)
    counts = jnp.sum(onehot, axis=0, keepdims=True)
    run = jnp.ceil(counts * (1.0 / RUN_ALIGN)) * RUN_ALIGN
    ei = lax.broadcasted_iota(jnp.int32, (LANES, LANES), 0)
    ej = lax.broadcasted_iota(jnp.int32, (LANES, LANES), 1)
    before = jnp.where(ei < ej, 1.0, 0.0)
    starts = jnp.dot(jnp.broadcast_to(run, (8, LANES)), before, precision=lax.Precision.HIGHEST,
                     preferred_element_type=jnp.float32)[0:1]
    slot = prefix + starts
    route = jnp.zeros(logits.shape, jnp.float32)
    gates = jnp.zeros(logits.shape, jnp.float32)
    for k in range(TOP_K):
        pos = jnp.sum(jnp.where(lane_f == idxs[k], slot, 0.0), axis=-1, keepdims=True)
        route = jnp.where(lane == k, idxs[k], route)
        route = jnp.where(lane == TOP_K + k, pos, route)
        gates = jnp.where(lane == k, exps[k] / denom, gates)
    route_ref[0] = route.astype(jnp.int32)
    gate_ref[0] = gates
    cnt_ref[0] = counts


def _out_route(o_diff, o_na, x, w_out, g1, g_ffn, sh2, sc2, w_router, b_router):
    b, s, d = x.shape
    tm = ROUTE_TM
    half = o_diff.shape[2]
    e = w_router.shape[1]
    wr = jnp.zeros((d, LANES), jnp.float32).at[:, :e].set(w_router)
    br = jnp.full((1, LANES), NEG, jnp.float32).at[0, :e].set(b_router)
    tile = lambda bi, i: (bi, i, 0)
    row = lambda bi, i: (0, 0)
    per_b = lambda bi, i: (bi, 0, 0)
    return pl.pallas_call(
        _out_route_kernel,
        out_shape=[jax.ShapeDtypeStruct((b, s, d), jnp.float32),
                   jax.ShapeDtypeStruct((b, s, d), jnp.bfloat16),
                   jax.ShapeDtypeStruct((b, s, LANES), jnp.int32),
                   jax.ShapeDtypeStruct((b, s, LANES), jnp.float32),
                   jax.ShapeDtypeStruct((b * (s // tm), 1, LANES), jnp.float32)],
        grid=(b, s // tm),
        in_specs=[pl.BlockSpec((1, tm, half), tile),
                  pl.BlockSpec((1, tm, half), tile),
                  pl.BlockSpec((1, tm, d), tile),
                  pl.BlockSpec(w_out.shape, row),
                  pl.BlockSpec((1, 1, d), per_b),
                  pl.BlockSpec((1, d), row),
                  pl.BlockSpec((1, 1, d), per_b),
                  pl.BlockSpec((1, 1, d), per_b),
                  pl.BlockSpec((d, LANES), row),
                  pl.BlockSpec((1, LANES), row)],
        out_specs=[pl.BlockSpec((1, tm, d), tile),
                   pl.BlockSpec((1, tm, d), tile),
                   pl.BlockSpec((1, tm, LANES), tile),
                   pl.BlockSpec((1, tm, LANES), tile),
                   pl.BlockSpec((1, 1, LANES), lambda bi, i: (bi * (s // tm) + i, 0, 0))],
        compiler_params=pltpu.CompilerParams(vmem_limit_bytes=VMEM_LIMIT),
        name="out_route",
    )(o_diff, o_na, x, w_out, g1, g_ffn, sh2, sc2, wr, br)


def _chunk_copies(tile, used_ref, dest_ref, local_ref, global_ref, sem, *, to_global, wait):
    n_chunks = local_ref.shape[0] // RUN_ALIGN

    def one(c, carry):
        lref = local_ref.at[pl.ds(pl.multiple_of(c * RUN_ALIGN, RUN_ALIGN), RUN_ALIGN), :]
        gref = global_ref.at[pl.ds(pl.multiple_of(dest_ref[tile * n_chunks + c], RUN_ALIGN), RUN_ALIGN), :]
        cp = pltpu.make_async_copy(lref, gref, sem) if to_global else pltpu.make_async_copy(gref, lref, sem)
        cp.wait() if wait else cp.start()
        return carry

    lax.fori_loop(0, used_ref[tile], one, 0)


def _slot_onehot(route, k, n_slots):
    slots = lax.broadcasted_iota(jnp.int32, (route.shape[0], n_slots), 1)
    return slots == route[:, TOP_K + k:TOP_K + k + 1]


def _dispatch_kernel(used_ref, dest_ref, route_ref, h_ref, xs_in_ref, xs_ref, buf_ref, sem):
    del xs_in_ref
    tile = pl.program_id(0)
    route = route_ref[...]
    n_slots = buf_ref.shape[0]
    sel = _slot_onehot(route, 0, n_slots)
    for k in range(1, TOP_K):
        sel = sel | _slot_onehot(route, k, n_slots)
    perm = jnp.where(sel, 1.0, 0.0).astype(jnp.bfloat16)
    rows = lax.dot_general(perm, h_ref[...], (((0,), (0,)), ((), ())),
                           preferred_element_type=jnp.float32)
    buf_ref[...] = rows.astype(buf_ref.dtype)
    args = (tile, used_ref, dest_ref, buf_ref, xs_ref, sem)
    _chunk_copies(*args, to_global=True, wait=False)
    _chunk_copies(*args, to_global=True, wait=True)


def _dispatch(used, dest, route, h2, n_rows):
    t, d = h2.shape
    tm = ROUTE_TM
    xs0 = jnp.zeros((n_rows, d), h2.dtype)
    grid_spec = pltpu.PrefetchScalarGridSpec(
        num_scalar_prefetch=2,
        grid=(t // tm,),
        in_specs=[pl.BlockSpec((tm, LANES), lambda i, *_: (i, 0)),
                  pl.BlockSpec((tm, d), lambda i, *_: (i, 0)),
                  pl.BlockSpec(memory_space=pl.ANY)],
        out_specs=pl.BlockSpec(memory_space=pl.ANY),
        scratch_shapes=[pltpu.VMEM((TILE_SLOTS, d), h2.dtype), pltpu.SemaphoreType.DMA(())],
    )
    return pl.pallas_call(
        _dispatch_kernel,
        out_shape=jax.ShapeDtypeStruct((n_rows, d), h2.dtype),
        grid_spec=grid_spec,
        input_output_aliases={4: 0},
        compiler_params=pltpu.CompilerParams(vmem_limit_bytes=VMEM_LIMIT),
        name="dispatch",
    )(used, dest, route, h2, xs0)


def _experts_kernel(be_ref, nv_ref, xs_ref, wgu_ref, bgu_ref, wdn_ref, bdn_ref, ys_ref, wgu_b_ref, wdn_b_ref):
    i = pl.program_id(0)
    valid = i < nv_ref[0]
    new_expert = (i == 0) | (be_ref[i] != be_ref[jnp.maximum(i - 1, 0)])

    @pl.when(valid & new_expert)
    def _():
        wgu_b_ref[...] = wgu_ref[0].astype(jnp.bfloat16)
        wdn_b_ref[...] = wdn_ref[0].astype(jnp.bfloat16)

    @pl.when(valid)
    def _():
        gu = jnp.dot(xs_ref[...], wgu_b_ref[...], preferred_element_type=jnp.float32) + bgu_ref[0]
        f = gu.shape[1] // 2
        glu = jnp.minimum(gu[:, :f], SWIGLU_LIMIT)
        lin = jnp.clip(gu[:, f:], -SWIGLU_LIMIT, SWIGLU_LIMIT)
        act = glu * (1.0 / (1.0 + jnp.exp(-SWIGLU_ALPHA * glu))) * (lin + 1.0)
        y = jnp.dot(act.astype(jnp.bfloat16), wdn_b_ref[...], preferred_element_type=jnp.float32) + bdn_ref[0]
        ys_ref[...] = y.astype(ys_ref.dtype)

    @pl.when(jnp.logical_not(valid))
    def _():
        ys_ref[...] = jnp.zeros(ys_ref.shape, ys_ref.dtype)


def _experts(block_expert, n_valid, xs, w_gu, b_gu, w_dn, b_dn):
    n_rows, dw = xs.shape
    tm = EXPERT_TM
    e, d, f2 = w_gu.shape
    grid_spec = pltpu.PrefetchScalarGridSpec(
        num_scalar_prefetch=2,
        grid=(n_rows // tm,),
        in_specs=[pl.BlockSpec((tm, dw), lambda i, be, nv: (i, 0)),
                  pl.BlockSpec((1, d, f2), lambda i, be, nv: (be[i], 0, 0)),
                  pl.BlockSpec((1, 1, f2), lambda i, be, nv: (be[i], 0, 0)),
                  pl.BlockSpec((1, f2 // 2, d), lambda i, be, nv: (be[i], 0, 0)),
                  pl.BlockSpec((1, 1, d), lambda i, be, nv: (be[i], 0, 0))],
        out_specs=pl.BlockSpec((tm, d), lambda i, be, nv: (i, 0)),
        scratch_shapes=[pltpu.VMEM((d, f2), jnp.bfloat16), pltpu.VMEM((f2 // 2, d), jnp.bfloat16)],
    )
    return pl.pallas_call(
        _experts_kernel,
        out_shape=jax.ShapeDtypeStruct((n_rows, d), jnp.bfloat16),
        grid_spec=grid_spec,
        compiler_params=pltpu.CompilerParams(vmem_limit_bytes=EXPERTS_VMEM_LIMIT),
        name="experts",
    )(block_expert, n_valid, xs, w_gu, b_gu.reshape(e, 1, f2), w_dn, b_dn.reshape(e, 1, d))


def _combine_kernel(used_ref, dest_ref, route_ref, gate_ref, x1_ref, g2_ref, ys_ref, o_ref, buf_ref, sem):
    tile = pl.program_id(0)
    covered = route_ref.shape[0] * TOP_K
    buf_ref[covered:] = jnp.zeros((buf_ref.shape[0] - covered, buf_ref.shape[1]), buf_ref.dtype)
    args = (tile, used_ref, dest_ref, buf_ref, ys_ref, sem)
    _chunk_copies(*args, to_global=False, wait=False)
    route = route_ref[...]
    gates = gate_ref[...]
    n_slots = buf_ref.shape[0]
    weights = jnp.zeros((route.shape[0], n_slots), jnp.float32)
    for k in range(TOP_K):
        weights = jnp.where(_slot_onehot(route, k, n_slots), gates[:, k:k + 1], weights)
    _chunk_copies(*args, to_global=False, wait=True)
    moe = jnp.dot(weights.astype(jnp.bfloat16), buf_ref[...], preferred_element_type=jnp.float32)
    o_ref[...] = x1_ref[...] + g2_ref[0] * moe


def _combine(used, dest, route, gates, x1, g2, ys, tiles_per_batch):
    t, d = x1.shape
    tm = ROUTE_TM
    grid_spec = pltpu.PrefetchScalarGridSpec(
        num_scalar_prefetch=2,
        grid=(t // tm,),
        in_specs=[pl.BlockSpec((tm, LANES), lambda i, *_: (i, 0)),
                  pl.BlockSpec((tm, LANES), lambda i, *_: (i, 0)),
                  pl.BlockSpec((tm, d), lambda i, *_: (i, 0)),
                  pl.BlockSpec((1, 1, d), lambda i, *_: (i // tiles_per_batch, 0, 0)),
                  pl.BlockSpec(memory_space=pl.ANY)],
        out_specs=pl.BlockSpec((tm, d), lambda i, *_: (i, 0)),
        scratch_shapes=[pltpu.VMEM((TILE_SLOTS, d), ys.dtype), pltpu.SemaphoreType.DMA(())],
    )
    return pl.pallas_call(
        _combine_kernel,
        out_shape=jax.ShapeDtypeStruct((t, d), jnp.float32),
        grid_spec=grid_spec,
        compiler_params=pltpu.CompilerParams(vmem_limit_bytes=VMEM_LIMIT),
        name="combine",
    )(used, dest, route, gates, x1, g2, ys)


def _score_bound(q_gain, k_gain):
    rounding = 1.02
    return HEAD_DIM * jnp.max(jnp.abs(q_gain)) * jnp.max(jnp.abs(k_gain)) * Q_SCALE * rounding

def _rope_tables(s):
    pos = jnp.arange(s, dtype=jnp.int32)
    inv = 1.0 / (ROPE_BASE ** (jnp.arange(ROPE_FREQS, dtype=jnp.float32) / ROPE_FREQS))
    ang_r = (pos // GRID_W).astype(jnp.float32)[:, None] * inv
    ang_c = (pos % GRID_W).astype(jnp.float32)[:, None] * inv
    ang = jnp.concatenate([ang_r, ang_r, ang_c, ang_c], axis=-1)
    sign = jnp.asarray(np.tile(np.repeat([-1.0, 1.0], ROPE_FREQS), 2), jnp.float32)
    cos_t = jnp.tile(jnp.cos(ang), (1, LANES // HEAD_DIM))
    sin_t = jnp.tile(jnp.sin(ang) * sign, (1, LANES // HEAD_DIM))
    return cos_t, sin_t


def kernel(x, c, ctx, c_ctx, w_ada, b_ada, g_attn, w_in, q_norm_diff, k_norm_diff, lam_q1, lam_k1, lam_q2,
           lam_k2, subln_diff, q_norm_na, k_norm_na, rpb_na, out_norm_na, w_out, g_ffn, w_router, b_router,
           w_gate_up, b_gate_up, w_down, b_down):
    depth = w_ada.shape[0]
    assert depth == 1, "single-layer kernel"
    b, s, d = x.shape
    n_ctx = ctx.shape[1]
    rows_n = s // GRID_W
    assert rows_n >= NA_ROWS and s % PROJ_TM == 0 and rows_n % NA_ROWS_PER_STEP == 0
    lam_init = 0.8 - 0.6 * math.exp(-0.3 * 0)

    rows = -(-(b + 1) // 8) * 8
    cv = jnp.zeros((rows, d), jnp.float32).at[:b].set(c).at[b].set(c_ctx)
    mod = _adaln(cv, w_ada[0], b_ada[0]).reshape(rows, 6, d)
    lat = [mod[:b, i][:, None, :] for i in range(6)]
    cxm = [jnp.broadcast_to(mod[b, i][None, None, :], (b, 1, d)) for i in range(6)]
    sh1, sc1, g1, sh2, sc2, g2 = lat

    w_in_b = w_in[0].astype(jnp.bfloat16)
    gidx = np.arange(256) // HEAD_DIM
    gsum = jnp.asarray(gidx[:, None] == gidx[None, :], jnp.bfloat16)
    tile4 = lambda v, reps: jnp.tile(v.reshape(1, -1), (1, reps))
    norms = (tile4(q_norm_diff[0], 8), tile4(k_norm_diff[0], 8), tile4(q_norm_na[0], 8), tile4(k_norm_na[0], 8))
    cos_t, sin_t = _rope_tables(s)
    g_attn2 = g_attn[0].reshape(1, d)

    qd, kd, vdt, qn, kn, vn = _proj(x, g_attn2, sh1, sc1, w_in_b, gsum, cos_t, sin_t, norms,
                                    groups=("qd", "kd", "vd", "qn", "kn", "vn"), rope=True, tm=PROJ_TM)
    kd_c, vdt_c, kn_c, vn_c = _proj(ctx, g_attn2, cxm[0], cxm[1], w_in_b, gsum, cos_t[:n_ctx], sin_t[:n_ctx],
                                    norms, groups=("kd", "vd", "kn", "vn"), rope=False, tm=n_ctx)

    lams = tuple(v[0].reshape(1, HEAD_DIM) for v in (lam_q1, lam_k1, lam_q2, lam_k2))
    diff_args = (qd, kd, kd_c, vdt, vdt_c, lams, subln_diff[0].reshape(1, LANES))
    o_diff = lax.cond(
        _score_bound(q_norm_diff[0], k_norm_diff[0]) <= EXP2_SAFE_SCORE,
        lambda a: _diff_attn(*a, lam_init=lam_init, stabilise=False),
        lambda a: _diff_attn(*a, lam_init=lam_init, stabilise=True), diff_args)
    na_args = (qn, kn, vn, kn_c, vn_c, _na_bias_table(rpb_na[0]), out_norm_na[0].reshape(1, -1))
    na_bound = _score_bound(q_norm_na[0], k_norm_na[0]) + jnp.max(jnp.abs(rpb_na[0])) * LOG2E
    o_na = lax.cond(
        na_bound <= EXP2_SAFE_SCORE,
        lambda a: _na_attn(*a, stabilise=False),
        lambda a: _na_attn(*a, stabilise=True), na_args)

    x1, h2, route, gates, counts = _out_route(o_diff, o_na, x, w_out[0].astype(jnp.bfloat16), g1,
                                              g_ffn[0].reshape(1, d), sh2, sc2, w_router[0], b_router[0])

    t = b * s
    n_exp = w_router.shape[2]
    assert n_exp == N_EXPERTS
    tile_cnt = counts[:, 0, :n_exp].astype(jnp.int32)
    tile_run = (tile_cnt + RUN_ALIGN - 1) // RUN_ALIGN * RUN_ALIGN
    totals = jnp.sum(tile_run, axis=0)
    padded = (totals + EXPERT_TM - 1) // EXPERT_TM * EXPERT_TM
    pad_ends = jnp.cumsum(padded)
    pad_starts = pad_ends - padded
    tile_loc = jnp.cumsum(tile_run, axis=1) - tile_run
    tile_glob = pad_starts[None, :] + jnp.cumsum(tile_run, axis=0) - tile_run
    max_rows = t * TOP_K + tile_cnt.size * (RUN_ALIGN - 1)
    n_tiles = -(-max_rows // EXPERT_TM) + n_exp
    tile_row0 = jnp.arange(n_tiles, dtype=jnp.int32) * EXPERT_TM
    block_expert = jnp.minimum(jnp.sum(tile_row0[:, None] >= pad_ends[None, :], axis=1), n_exp - 1).astype(jnp.int32)
    n_valid = (pad_ends[-1:] // EXPERT_TM).astype(jnp.int32)
    run_end = tile_loc + tile_run
    chunk_row = jnp.arange(TILE_SLOTS // RUN_ALIGN, dtype=jnp.int32) * RUN_ALIGN
    chunk_expert = jnp.minimum(jnp.sum(chunk_row[None, :, None] >= run_end[:, None, :], axis=-1), n_exp - 1)
    chunk_dest = jnp.take_along_axis(tile_glob - tile_loc, chunk_expert, axis=1) + chunk_row[None, :]
    chunks_used = run_end[:, -1] // RUN_ALIGN
    moves = (chunks_used.astype(jnp.int32), chunk_dest.reshape(-1).astype(jnp.int32), route.reshape(t, LANES))

    xs = _dispatch(*moves, h2.reshape(t, d), n_tiles * EXPERT_TM)
    ys = _experts(block_expert, n_valid, xs, w_gate_up[0], b_gate_up[0], w_down[0], b_down[0])
    out = _combine(*moves, gates.reshape(t, LANES), x1.reshape(t, d), g2, ys, s // ROUTE_TM)
    return out.reshape(b, s, d)
```

```python
import functools
import math

import jax
import jax.numpy as jnp
import numpy as np
from jax import lax
from jax.experimental import pallas as pl
from jax.experimental.pallas import tpu as pltpu

GRID_W = 64
HEAD_DIM = 64
NA_ROWS = 8
NA_COLS = 16
ROPE_BASE = 10000.0
ROPE_FREQS = HEAD_DIM // 4
N_EXPERTS = 32
TOP_K = 4
SWIGLU_ALPHA = 1.702
SWIGLU_LIMIT = 7.0
NORM_EPS = 1e-6

LANES = 128
NEG = -1e30
LOG2E = math.log2(math.e)
Q_SCALE = HEAD_DIM ** -0.5 * LOG2E
EXP2_SAFE_SCORE = 50.0

PROJ_TM = 512
DIFF_TQ = 256
DIFF_STATIC_KLOOP = True
DIFF_SCORE_BUFS = 4
NA_ROWS_PER_STEP = 16
NA_ROW_UNROLL = 8
ROUTE_TM = 512
EXPERT_TM = 512
RUN_ALIGN = 16
TILE_SLOTS = ROUTE_TM * TOP_K + N_EXPERTS * RUN_ALIGN
VMEM_LIMIT = 48 * 1024 * 1024
EXPERTS_VMEM_LIMIT = 56 * 1024 * 1024

_NT = (((1,), (1,)), ((), ()))


def _f32(x):
    return x.astype(jnp.float32)


def _adaln_kernel(cv_ref, w_ref, b_ref, o_ref):
    cv = cv_ref[...]
    act = cv * (1.0 / (1.0 + jnp.exp(-cv)))
    o_ref[...] = jnp.dot(act, w_ref[...], precision=lax.Precision.HIGHEST,
                         preferred_element_type=jnp.float32) + b_ref[...]


def _adaln(cv, w_ada, b_ada):
    rows, d = cv.shape
    n = w_ada.shape[1]
    tn = 1024
    return pl.pallas_call(
        _adaln_kernel,
        out_shape=jax.ShapeDtypeStruct((rows, n), jnp.float32),
        grid=(n // tn,),
        in_specs=[pl.BlockSpec((rows, d), lambda j: (0, 0)),
                  pl.BlockSpec((d, tn), lambda j: (0, j)),
                  pl.BlockSpec((1, tn), lambda j: (0, j))],
        out_specs=pl.BlockSpec((rows, tn), lambda j: (0, j)),
        compiler_params=pltpu.CompilerParams(vmem_limit_bytes=VMEM_LIMIT),
        name="adaln",
    )(cv, w_ada, b_ada.reshape(1, n))


def _proj_kernel(*refs, groups, rope, d_model):
    (x_ref, g_ref, sh_ref, sc_ref, w_ref, gsum_ref, cos_ref, sin_ref,
     qnd_ref, knd_ref, qnn_ref, knn_ref) = refs[:12]
    outs = dict(zip(groups, refs[12:]))
    width = 4 * LANES

    x = x_ref[0]
    y = x * lax.rsqrt(jnp.mean(x * x, axis=-1, keepdims=True) + NORM_EPS) * g_ref[...]
    h = (y * (1.0 + sc_ref[0]) + sh_ref[0]).astype(jnp.bfloat16)

    group_col = {"qd": 0, "kd": 1, "vd": 2, "qn": 3, "kn": 4, "vn": 5}
    gains = {"qd": qnd_ref[...] * Q_SCALE, "kd": knd_ref[...],
             "qn": qnn_ref[...] * Q_SCALE, "kn": knn_ref[...]}
    lane = lax.broadcasted_iota(jnp.int32, (x.shape[0], LANES), 1)
    first_half = (lane % (2 * ROPE_FREQS)) < ROPE_FREQS

    for name in groups:
        c0 = group_col[name] * width
        p = jnp.dot(h, w_ref[:, c0:c0 + width], preferred_element_type=jnp.float32)
        if name in gains:
            blocks = []
            for j in range(width // 256):
                blk = p[:, j * 256:(j + 1) * 256]
                ss = jnp.dot((blk * blk).astype(jnp.bfloat16), gsum_ref[...],
                             preferred_element_type=jnp.float32)
                blocks.append(blk * lax.rsqrt(ss * (1.0 / HEAD_DIM) + NORM_EPS))
            p = jnp.concatenate(blocks, axis=1) * gains[name]
        if rope and name in ("qd", "kd"):
            blocks = []
            for j in range(width // LANES):
                blk = p[:, j * LANES:(j + 1) * LANES]
                partner = jnp.where(first_half,
                                    pltpu.roll(blk, LANES - ROPE_FREQS, axis=1),
                                    pltpu.roll(blk, ROPE_FREQS, axis=1))
                blocks.append(blk * cos_ref[...] + partner * sin_ref[...])
            p = jnp.concatenate(blocks, axis=1)
        if name == "vd":
            pt = p.T.reshape(width // LANES, LANES, p.shape[0])
            outs[name][0, :, 0] = pt.astype(jnp.bfloat16)
        else:
            outs[name][0] = p.astype(jnp.bfloat16)


def _proj(x, g_attn, shift, scale, w_in, gsum, cos_t, sin_t, norms, *, groups, rope, tm):
    b, s, d = x.shape
    nt = s // tm
    width = 4 * LANES
    out_shape, out_specs = [], []
    for name in groups:
        if name == "vd":
            out_shape.append(jax.ShapeDtypeStruct((b, 4, nt, LANES, tm), jnp.bfloat16))
            out_specs.append(pl.BlockSpec((1, 4, 1, LANES, tm), lambda bi, i: (bi, 0, i, 0, 0)))
        else:
            out_shape.append(jax.ShapeDtypeStruct((b, s, width), jnp.bfloat16))
            out_specs.append(pl.BlockSpec((1, tm, width), lambda bi, i: (bi, i, 0)))
    row = lambda bi, i: (0, 0)
    per_b = lambda bi, i: (bi, 0, 0)
    kern = functools.partial(_proj_kernel, groups=groups, rope=rope, d_model=d)
    return pl.pallas_call(
        kern,
        out_shape=out_shape,
        grid=(b, nt),
        in_specs=[pl.BlockSpec((1, tm, d), lambda bi, i: (bi, i, 0)),
                  pl.BlockSpec((1, d), row),
                  pl.BlockSpec((1, 1, d), per_b),
                  pl.BlockSpec((1, 1, d), per_b),
                  pl.BlockSpec(w_in.shape, row),
                  pl.BlockSpec(gsum.shape, row),
                  pl.BlockSpec((tm, LANES), lambda bi, i: (i, 0)),
                  pl.BlockSpec((tm, LANES), lambda bi, i: (i, 0)),
                  pl.BlockSpec((1, width), row), pl.BlockSpec((1, width), row),
                  pl.BlockSpec((1, width), row), pl.BlockSpec((1, width), row)],
        out_specs=out_specs,
        compiler_params=pltpu.CompilerParams(vmem_limit_bytes=VMEM_LIMIT),
        name="proj_rope" if rope else "proj_ctx",
    )(x, g_attn, shift, scale, w_in, gsum, cos_t, sin_t, *norms)


def _diff_attn_kernel(q_ref, k_ref, kc_ref, vt_ref, vtc_ref, lq1_ref, lk1_ref, lq2_ref, lk2_ref,
                      subln_ref, o_ref, m_ref, l_ref, acc_ref, *s_refs, lam_init, n_kblocks, stabilise):
    q = q_ref[0]
    lane = lax.broadcasted_iota(jnp.int32, q.shape, 1)
    zero = jnp.zeros_like(q)
    qz = (jnp.where(lane < HEAD_DIM, q, zero), jnp.where(lane >= HEAD_DIM, q, zero))

    l_ref[...] = jnp.zeros(l_ref.shape, jnp.float32)
    acc_ref[...] = jnp.zeros(acc_ref.shape, jnp.float32)

    def sum8(p):
        return jnp.sum(p.reshape(p.shape[0] // 8, 8, p.shape[1]), axis=0)

    def scores(k, mp):
        return lax.dot_general(k, qz[mp], _NT, preferred_element_type=jnp.float32)

    if stabilise:
        m_ref[...] = jnp.full(m_ref.shape, NEG, jnp.float32)

        def block(k, vt):
            for mp in range(2):
                s = scores(k, mp)
                m_old = m_ref[mp]
                m_new = jnp.maximum(m_old, jnp.max(s, axis=0, keepdims=True))
                alpha = jnp.exp2(m_old - m_new)
                p = jnp.exp2(s - m_new)
                l_ref[mp] = alpha * l_ref[mp] + sum8(p)
                acc_ref[mp] = alpha * acc_ref[mp] + jnp.dot(vt, p.astype(jnp.bfloat16),
                                                            preferred_element_type=jnp.float32)
                m_ref[mp] = m_new

        def body(i, carry):
            block(k_ref[0, i], vt_ref[0, 0, i])
            return carry

        lax.fori_loop(0, n_kblocks, body, 0)
        block(kc_ref[0], vtc_ref[0, 0, 0])
    else:
        n_bufs = len(s_refs)
        ahead = n_bufs // 2

        def produce_into(k, s_ref):
            for mp in range(2):
                s_ref[mp, :k.shape[0]] = scores(k, mp)

        def consume(s_ref, vt):
            for mp in range(2):
                p = jnp.exp2(s_ref[mp, :vt.shape[1]])
                l_ref[mp] += sum8(p)
                acc_ref[mp] += jnp.dot(vt, p.astype(jnp.bfloat16), preferred_element_type=jnp.float32)

        for i in range(ahead):
            produce_into(k_ref[0, i], s_refs[i])

        def body(j, carry):
            base = n_bufs * j
            for u in range(n_bufs):
                produce_into(k_ref[0, base + u + ahead], s_refs[(u + ahead) % n_bufs])
                consume(s_refs[u], vt_ref[0, 0, base + u])
            return carry

        n_main = 0 if DIFF_STATIC_KLOOP else (n_kblocks - ahead) // n_bufs
        lax.fori_loop(0, n_main, body, 0)
        for i in range(n_main * n_bufs, n_kblocks + 1):
            nxt = i + ahead
            if nxt <= n_kblocks:
                produce_into(kc_ref[0] if nxt == n_kblocks else k_ref[0, nxt], s_refs[nxt % n_bufs])
            consume(s_refs[i % n_bufs], vtc_ref[0, 0, 0] if i == n_kblocks else vt_ref[0, 0, i])

    lam = (jnp.exp(jnp.sum(lq1_ref[...] * lk1_ref[...], keepdims=True))
           - jnp.exp(jnp.sum(lq2_ref[...] * lk2_ref[...], keepdims=True)) + lam_init)
    l1 = jnp.sum(l_ref[0], axis=0, keepdims=True)
    l2 = jnp.sum(l_ref[1], axis=0, keepdims=True)
    o = acc_ref[0] / l1 - lam * (acc_ref[1] / l2)
    ot = o.T
    ot = ot * lax.rsqrt(jnp.mean(ot * ot, axis=-1, keepdims=True) + NORM_EPS)
    o_ref[0] = (ot * subln_ref[...] * (1.0 - lam_init)).astype(o_ref.dtype)


def _diff_attn(qd, kd, kd_c, vdt, vdt_c, lams, subln, *, lam_init, stabilise):
    b, s, width = qd.shape
    heads = width // LANES
    tk = vdt.shape[-1]
    nkb = s // tk
    c = kd_c.shape[1]
    kd4 = kd.reshape(b, nkb, tk, width)
    tq = DIFF_TQ
    assert nkb >= DIFF_SCORE_BUFS // 2 and c <= tk
    kern = functools.partial(_diff_attn_kernel, lam_init=lam_init, n_kblocks=nkb, stabilise=stabilise)
    vec = pl.BlockSpec((1, HEAD_DIM), lambda bi, h, i: (0, 0))
    return pl.pallas_call(
        kern,
        out_shape=jax.ShapeDtypeStruct((b, s, width), jnp.bfloat16),
        grid=(b, heads, s // tq),
        in_specs=[pl.BlockSpec((1, tq, LANES), lambda bi, h, i: (bi, i, h)),
                  pl.BlockSpec((1, nkb, tk, LANES), lambda bi, h, i: (bi, 0, 0, h)),
                  pl.BlockSpec((1, c, LANES), lambda bi, h, i: (bi, 0, h)),
                  pl.BlockSpec((1, 1, nkb, LANES, tk), lambda bi, h, i: (bi, h, 0, 0, 0)),
                  pl.BlockSpec((1, 1, 1, LANES, c), lambda bi, h, i: (bi, h, 0, 0, 0)),
                  vec, vec, vec, vec,
                  pl.BlockSpec((1, LANES), lambda bi, h, i: (0, 0))],
        out_specs=pl.BlockSpec((1, tq, LANES), lambda bi, h, i: (bi, i, h)),
        scratch_shapes=[pltpu.VMEM((2, 1, tq), jnp.float32),
                        pltpu.VMEM((2, 8, tq), jnp.float32),
                        pltpu.VMEM((2, LANES, tq), jnp.float32)]
                       + [pltpu.VMEM((2, tk, tq), jnp.float32)] * DIFF_SCORE_BUFS,
        compiler_params=pltpu.CompilerParams(vmem_limit_bytes=VMEM_LIMIT),
        name="diff_attn",
    )(qd, kd4, kd_c, vdt, vdt_c, *lams, subln)


def _na_bias_table(rpb):
    heads = rpb.shape[0]
    r = np.arange(2 * NA_ROWS - 1)[:, None, None]
    row_sel = (r == np.arange(NA_ROWS)[None, None, :] - np.arange(NA_ROWS)[None, :, None] + (NA_ROWS - 1))
    wq = np.arange(GRID_W)[:, None]
    wk = np.arange(GRID_W)[None, :]
    start = np.clip(wq - NA_COLS // 2, 0, GRID_W - NA_COLS)
    inside = (wk >= start) & (wk < start + NA_COLS)
    col_sel = (np.arange(2 * NA_COLS - 1)[:, None, None] == (wk - wq + (NA_COLS - 1))[None]) & inside[None]
    t = jnp.einsum("hrc,rdi,cqk->dhqik", rpb, jnp.asarray(row_sel, jnp.float32), jnp.asarray(col_sel, jnp.float32),
                   precision=lax.Precision.HIGHEST)
    t = jnp.where(inside[None, None, :, None, :], t * LOG2E, NEG)
    return t.reshape(NA_ROWS, heads // 2, 2 * GRID_W, NA_ROWS * GRID_W).astype(jnp.float32)


def _na_attn_kernel(q_ref, k_ref, v_ref, kc_ref, vc_ref, bias_ref, gain_ref, o_ref, octx_ref, lctx_ref, *,
                    rows_n, rows_per_step, stabilise):
    step = pl.program_id(2)
    win = NA_ROWS * GRID_W
    low = lax.broadcasted_iota(jnp.int32, (GRID_W, LANES), 1) < HEAD_DIM
    kc = kc_ref[0]
    vc = vc_ref[0]
    gain = gain_ref[...]

    def head_lanes(q, hh):
        keep = (lax.broadcasted_iota(jnp.int32, q.shape, 1) < HEAD_DIM) == (hh == 0)
        return jnp.where(keep, q, jnp.zeros_like(q))

    if not stabilise:
        q_all = q_ref[0]
        for hh in range(2):
            p = jnp.exp2(lax.dot_general(head_lanes(q_all, hh), kc, _NT, preferred_element_type=jnp.float32))
            lctx_ref[hh] = jnp.broadcast_to(jnp.sum(p, axis=-1, keepdims=True), lctx_ref.shape[1:])
            octx_ref[hh] = jnp.dot(p.astype(jnp.bfloat16), vc, preferred_element_type=jnp.float32)

    def row(j):
        r = step * rows_per_step + j
        r0 = jnp.clip(r - NA_ROWS // 2, 0, rows_n - NA_ROWS)
        d = r - r0
        start = pl.multiple_of(r0 * GRID_W, GRID_W)
        rows = pl.ds(pl.multiple_of(j * GRID_W, GRID_W), GRID_W)
        qrow = q_ref[0, rows, :]
        kw = k_ref[0, pl.ds(start, win), :]
        vw = v_ref[0, pl.ds(start, win), :]
        q2 = jnp.concatenate([head_lanes(qrow, 0), head_lanes(qrow, 1)], axis=0)
        s = lax.dot_general(q2, kw, _NT, preferred_element_type=jnp.float32) + bias_ref[d, 0]
        if stabilise:
            s_ctx = lax.dot_general(q2, kc, _NT, preferred_element_type=jnp.float32)
            m = jnp.maximum(jnp.max(s, axis=-1, keepdims=True), jnp.max(s_ctx, axis=-1, keepdims=True))
            p = jnp.exp2(s - m)
            p_ctx = jnp.exp2(s_ctx - m)
            l = jnp.sum(p, axis=-1, keepdims=True) + jnp.sum(p_ctx, axis=-1, keepdims=True)
            o2 = (jnp.dot(p.astype(jnp.bfloat16), vw, preferred_element_type=jnp.float32)
                  + jnp.dot(p_ctx.astype(jnp.bfloat16), vc, preferred_element_type=jnp.float32)) / l
            o = jnp.where(low, o2[:GRID_W], o2[GRID_W:])
        else:
            p = jnp.exp2(s)
            l = jnp.sum(p, axis=-1, keepdims=True)
            o2 = jnp.dot(p.astype(jnp.bfloat16), vw, preferred_element_type=jnp.float32)
            o = jnp.where(low,
                          (o2[:GRID_W] + octx_ref[0, rows, :]) / (l[:GRID_W] + lctx_ref[0, rows, :]),
                          (o2[GRID_W:] + octx_ref[1, rows, :]) / (l[GRID_W:] + lctx_ref[1, rows, :]))
        sq = o * o
        ms0 = jnp.sum(jnp.where(low, sq, 0.0), axis=-1, keepdims=True) * (1.0 / HEAD_DIM)
        ms1 = jnp.sum(jnp.where(low, 0.0, sq), axis=-1, keepdims=True) * (1.0 / HEAD_DIM)
        inv = jnp.where(low, lax.rsqrt(ms0 + NORM_EPS), lax.rsqrt(ms1 + NORM_EPS))
        o_ref[0, rows, :] = (o * inv * gain).astype(o_ref.dtype)

    def row_group(jj, carry):
        for u in range(NA_ROW_UNROLL):
            row(NA_ROW_UNROLL * jj + u)
        return carry

    lax.fori_loop(0, rows_per_step // NA_ROW_UNROLL, row_group, 0)


def _na_attn(qn, kn, vn, kn_c, vn_c, bias, out_gain, *, stabilise):
    b, s, width = qn.shape
    pairs = width // LANES
    rows_n = s // GRID_W
    rps = NA_ROWS_PER_STEP
    c = kn_c.shape[1]
    tq = rps * GRID_W
    kern = functools.partial(_na_attn_kernel, rows_n=rows_n, rows_per_step=rps, stabilise=stabilise)
    whole = lambda bi, h, i: (bi, 0, h)
    return pl.pallas_call(
        kern,
        out_shape=jax.ShapeDtypeStruct((b, s, width), jnp.bfloat16),
        grid=(b, pairs, rows_n // rps),
        in_specs=[pl.BlockSpec((1, tq, LANES), lambda bi, h, i: (bi, i, h)),
                  pl.BlockSpec((1, s, LANES), whole),
                  pl.BlockSpec((1, s, LANES), whole),
                  pl.BlockSpec((1, c, LANES), whole),
                  pl.BlockSpec((1, c, LANES), whole),
                  pl.BlockSpec((NA_ROWS, 1, 2 * GRID_W, NA_ROWS * GRID_W), lambda bi, h, i: (0, h, 0, 0)),
                  pl.BlockSpec((1, LANES), lambda bi, h, i: (0, h))],
        out_specs=pl.BlockSpec((1, tq, LANES), lambda bi, h, i: (bi, i, h)),
        scratch_shapes=[pltpu.VMEM((2, tq, LANES), jnp.float32), pltpu.VMEM((2, tq, LANES), jnp.float32)],
        compiler_params=pltpu.CompilerParams(vmem_limit_bytes=VMEM_LIMIT),
        name="na_attn",
    )(qn, kn, vn, kn_c, vn_c, bias, out_gain)


def _out_route_kernel(od_ref, on_ref, x_ref, w_ref, g1_ref, gf_ref, sh_ref, sc_ref, wr_ref, br_ref,
                      x1_ref, h2_ref, route_ref, gate_ref, cnt_ref):
    half = od_ref.shape[2]
    attn = (jnp.dot(od_ref[0], w_ref[:half], preferred_element_type=jnp.float32)
            + jnp.dot(on_ref[0], w_ref[half:], preferred_element_type=jnp.float32))
    x1 = x_ref[0] + g1_ref[0] * attn
    x1_ref[...] = x1
    y = x1 * lax.rsqrt(jnp.mean(x1 * x1, axis=-1, keepdims=True) + NORM_EPS) * gf_ref[...]
    h2 = y * (1.0 + sc_ref[0]) + sh_ref[0]
    h2_hi = h2.astype(jnp.bfloat16)
    h2_ref[...] = h2_hi

    h2_lo = (h2 - h2_hi.astype(jnp.float32)).astype(jnp.bfloat16)
    wr = wr_ref[...]
    wr_hi = wr.astype(jnp.bfloat16)
    wr_lo = (wr - wr_hi.astype(jnp.float32)).astype(jnp.bfloat16)
    logits = (jnp.dot(h2_hi, wr_hi, preferred_element_type=jnp.float32)
              + jnp.dot(h2_lo, wr_hi, preferred_element_type=jnp.float32)
              + jnp.dot(h2_hi, wr_lo, preferred_element_type=jnp.float32)) + br_ref[...]
    tm = logits.shape[0]
    lane = lax.broadcasted_iota(jnp.int32, logits.shape, 1)
    lane_f = lane.astype(jnp.float32)
    vals, idxs = [], []
    cur = logits
    for _ in range(TOP_K):
        mx = jnp.max(cur, axis=-1, keepdims=True)
        ik = jnp.min(jnp.where(cur == mx, lane_f, float(LANES)), axis=-1, keepdims=True)
        vals.append(mx)
        idxs.append(ik)
        cur = jnp.where(lane_f == ik, -3e38, cur)
    exps = [jnp.exp(v - vals[0]) for v in vals]
    denom = exps[0] + exps[1] + exps[2] + exps[3]

    onehot = jnp.zeros(logits.shape, jnp.float32)
    for ik in idxs:
        onehot = onehot + jnp.where(lane_f == ik, 1.0, 0.0)
    ti = lax.broadcasted_iota(jnp.int32, (tm, tm), 0)
    tj = lax.broadcasted_iota(jnp.int32, (tm, tm), 1)
    lower = jnp.where(tj < ti, 1.0, 0.0).astype(jnp.bfloat16)
    prefix = jnp.dot(lower, onehot.astype(jnp.bfloat16), preferred_element_type=jnp.float32)
    counts = jnp.sum(onehot, axis=0, keepdims=True)
    run = jnp.ceil(counts * (1.0 / RUN_ALIGN)) * RUN_ALIGN
    ei = lax.broadcasted_iota(jnp.int32, (LANES, LANES), 0)
    ej = lax.broadcasted_iota(jnp.int32, (LANES, LANES), 1)
    before = jnp.where(ei < ej, 1.0, 0.0)
    starts = jnp.dot(jnp.broadcast_to(run, (8, LANES)), before, precision=lax.Precision.HIGHEST,
                     preferred_element_type=jnp.float32)[0:1]
    slot = prefix + starts
    route = jnp.zeros(logits.shape, jnp.float32)
    gates = jnp.zeros(logits.shape, jnp.float32)
    for k in range(TOP_K):
        pos = jnp.sum(jnp.where(lane_f == idxs[k], slot, 0.0), axis=-1, keepdims=True)
        route = jnp.where(lane == k, idxs[k], route)
        route = jnp.where(lane == TOP_K + k, pos, route)
        gates = jnp.where(lane == k, exps[k] / denom, gates)
    route_ref[...] = route.astype(jnp.int32)
    gate_ref[...] = gates
    cnt_ref[0] = counts


def _out_route(o_diff, o_na, x, w_out, g1, g_ffn, sh2, sc2, w_router, b_router):
    b, s, d = x.shape
    tm = ROUTE_TM
    half = o_diff.shape[2]
    e = w_router.shape[1]
    wr = jnp.zeros((d, LANES), jnp.float32).at[:, :e].set(w_router)
    br = jnp.full((1, LANES), NEG, jnp.float32).at[0, :e].set(b_router)
    tile = lambda bi, i: (bi, i, 0)
    flat = lambda bi, i: (bi * (s // tm) + i, 0)
    row = lambda bi, i: (0, 0)
    per_b = lambda bi, i: (bi, 0, 0)
    return pl.pallas_call(
        _out_route_kernel,
        out_shape=[jax.ShapeDtypeStruct((b * s, d), jnp.float32),
                   jax.ShapeDtypeStruct((b * s, d), jnp.bfloat16),
                   jax.ShapeDtypeStruct((b * s, LANES), jnp.int32),
                   jax.ShapeDtypeStruct((b * s, LANES), jnp.float32),
                   jax.ShapeDtypeStruct((b * (s // tm), 1, LANES), jnp.float32)],
        grid=(b, s // tm),
        in_specs=[pl.BlockSpec((1, tm, half), tile),
                  pl.BlockSpec((1, tm, half), tile),
                  pl.BlockSpec((1, tm, d), tile),
                  pl.BlockSpec(w_out.shape, row),
                  pl.BlockSpec((1, 1, d), per_b),
                  pl.BlockSpec((1, d), row),
                  pl.BlockSpec((1, 1, d), per_b),
                  pl.BlockSpec((1, 1, d), per_b),
                  pl.BlockSpec((d, LANES), row),
                  pl.BlockSpec((1, LANES), row)],
        out_specs=[pl.BlockSpec((tm, d), flat),
                   pl.BlockSpec((tm, d), flat),
                   pl.BlockSpec((tm, LANES), flat),
                   pl.BlockSpec((tm, LANES), flat),
                   pl.BlockSpec((1, 1, LANES), lambda bi, i: (bi * (s // tm) + i, 0, 0))],
        compiler_params=pltpu.CompilerParams(vmem_limit_bytes=VMEM_LIMIT),
        name="out_route",
    )(o_diff, o_na, x, w_out, g1, g_ffn, sh2, sc2, wr, br)


def _chunk_copies(tile, used_ref, dest_ref, local_ref, global_ref, sem, *, to_global, wait):
    n_chunks = local_ref.shape[0] // RUN_ALIGN

    def one(c, carry):
        lref = local_ref.at[pl.ds(pl.multiple_of(c * RUN_ALIGN, RUN_ALIGN), RUN_ALIGN), :]
        gref = global_ref.at[pl.ds(pl.multiple_of(dest_ref[tile * n_chunks + c], RUN_ALIGN), RUN_ALIGN), :]
        cp = pltpu.make_async_copy(lref, gref, sem) if to_global else pltpu.make_async_copy(gref, lref, sem)
        cp.wait() if wait else cp.start()
        return carry

    lax.fori_loop(0, used_ref[tile], one, 0)


def _slot_onehot(route, k, n_slots):
    slots = lax.broadcasted_iota(jnp.int32, (route.shape[0], n_slots), 1)
    return slots == route[:, TOP_K + k:TOP_K + k + 1]


def _zero_unsorted_rows(pad_row_ref, pad_chunks_ref, nv_ref, zero_ref, xs_ref, sem):
    zero_ref[...] = jnp.zeros(zero_ref.shape, zero_ref.dtype)
    n_tiles = xs_ref.shape[0] // EXPERT_TM
    for wait in (False, True):
        def pad_chunk(e, j, c):
            row = pl.multiple_of(pad_row_ref[e] + j * RUN_ALIGN, RUN_ALIGN)
            cp = pltpu.make_async_copy(zero_ref.at[pl.ds(0, RUN_ALIGN), :], xs_ref.at[pl.ds(row, RUN_ALIGN), :], sem)
            cp.wait() if wait else cp.start()
            return c

        def unused_tile(i, c):
            row = pl.multiple_of(i * EXPERT_TM, EXPERT_TM)
            cp = pltpu.make_async_copy(zero_ref, xs_ref.at[pl.ds(row, EXPERT_TM), :], sem)
            cp.wait() if wait else cp.start()
            return c

        lax.fori_loop(0, N_EXPERTS, lambda e, c: lax.fori_loop(
            0, pad_chunks_ref[e], functools.partial(pad_chunk, e), c), 0)
        lax.fori_loop(nv_ref[0], n_tiles, unused_tile, 0)


def _dispatch_kernel(used_ref, dest_ref, pad_row_ref, pad_chunks_ref, nv_ref, route_ref, h_ref, xs_ref,
                     buf_ref, zero_ref, sems, zero_sem):
    tile = pl.program_id(0)
    slot = tile % 2

    @pl.when(tile == 0)
    def _():
        _zero_unsorted_rows(pad_row_ref, pad_chunks_ref, nv_ref, zero_ref, xs_ref, zero_sem)

    route = route_ref[...]
    n_slots = buf_ref.shape[1]
    sel = _slot_onehot(route, 0, n_slots)
    for k in range(1, TOP_K):
        sel = sel | _slot_onehot(route, k, n_slots)
    perm = jnp.where(sel, 1.0, 0.0).astype(jnp.bfloat16)
    rows = lax.dot_general(perm, h_ref[...], (((0,), (0,)), ((), ())),
                           preferred_element_type=jnp.float32)
    buf_ref[slot] = rows.astype(buf_ref.dtype)

    def copies(t, s, wait):
        _chunk_copies(t, used_ref, dest_ref, buf_ref.at[s], xs_ref, sems.at[s], to_global=True, wait=wait)

    copies(tile, slot, False)

    @pl.when(tile > 0)
    def _():
        copies(tile - 1, 1 - slot, True)

    @pl.when(tile == pl.num_programs(0) - 1)
    def _():
        copies(tile, slot, True)


def _dispatch(used, dest, pad_row, pad_chunks, n_valid, route, h2, n_rows):
    t, d = h2.shape
    tm = ROUTE_TM
    grid_spec = pltpu.PrefetchScalarGridSpec(
        num_scalar_prefetch=5,
        grid=(t // tm,),
        in_specs=[pl.BlockSpec((tm, LANES), lambda i, *_: (i, 0)),
                  pl.BlockSpec((tm, d), lambda i, *_: (i, 0))],
        out_specs=pl.BlockSpec(memory_space=pl.ANY),
        scratch_shapes=[pltpu.VMEM((2, TILE_SLOTS, d), h2.dtype), pltpu.VMEM((EXPERT_TM, d), h2.dtype),
                        pltpu.SemaphoreType.DMA((2,)), pltpu.SemaphoreType.DMA(())],
    )
    return pl.pallas_call(
        _dispatch_kernel,
        out_shape=jax.ShapeDtypeStruct((n_rows, d), h2.dtype),
        grid_spec=grid_spec,
        compiler_params=pltpu.CompilerParams(vmem_limit_bytes=VMEM_LIMIT),
        name="dispatch",
    )(used, dest, pad_row, pad_chunks, n_valid, route, h2)


def _experts_kernel(be_ref, nv_ref, xs_ref, wgu_ref, bgu_ref, wdn_ref, bdn_ref, ys_ref, wgu_b_ref, wdn_b_ref):
    i = pl.program_id(0)
    valid = i < nv_ref[0]
    new_expert = (i == 0) | (be_ref[i] != be_ref[jnp.maximum(i - 1, 0)])

    @pl.when(valid & new_expert)
    def _():
        wgu_b_ref[...] = wgu_ref[0].astype(jnp.bfloat16)
        wdn_b_ref[...] = wdn_ref[0].astype(jnp.bfloat16)

    @pl.when(valid)
    def _():
        gu = jnp.dot(xs_ref[...], wgu_b_ref[...], preferred_element_type=jnp.float32) + bgu_ref[0]
        f = gu.shape[1] // 2
        glu = jnp.minimum(gu[:, :f], SWIGLU_LIMIT)
        lin = jnp.clip(gu[:, f:], -SWIGLU_LIMIT, SWIGLU_LIMIT)
        act = glu * (1.0 / (1.0 + jnp.exp(-SWIGLU_ALPHA * glu))) * (lin + 1.0)
        y = jnp.dot(act.astype(jnp.bfloat16), wdn_b_ref[...], preferred_element_type=jnp.float32) + bdn_ref[0]
        ys_ref[...] = y.astype(ys_ref.dtype)

    @pl.when(jnp.logical_not(valid))
    def _():
        ys_ref[...] = jnp.zeros(ys_ref.shape, ys_ref.dtype)


def _experts(block_expert, n_valid, xs, w_gu, b_gu, w_dn, b_dn):
    n_rows, dw = xs.shape
    tm = EXPERT_TM
    e, d, f2 = w_gu.shape
    grid_spec = pltpu.PrefetchScalarGridSpec(
        num_scalar_prefetch=2,
        grid=(n_rows // tm,),
        in_specs=[pl.BlockSpec((tm, dw), lambda i, be, nv: (jnp.minimum(i, nv[0] - 1), 0)),
                  pl.BlockSpec((1, d, f2), lambda i, be, nv: (be[i], 0, 0)),
                  pl.BlockSpec((1, 1, f2), lambda i, be, nv: (be[i], 0, 0)),
                  pl.BlockSpec((1, f2 // 2, d), lambda i, be, nv: (be[i], 0, 0)),
                  pl.BlockSpec((1, 1, d), lambda i, be, nv: (be[i], 0, 0))],
        out_specs=pl.BlockSpec((tm, d), lambda i, be, nv: (i, 0)),
        scratch_shapes=[pltpu.VMEM((d, f2), jnp.bfloat16), pltpu.VMEM((f2 // 2, d), jnp.bfloat16)],
    )
    return pl.pallas_call(
        _experts_kernel,
        out_shape=jax.ShapeDtypeStruct((n_rows, d), jnp.bfloat16),
        grid_spec=grid_spec,
        compiler_params=pltpu.CompilerParams(vmem_limit_bytes=EXPERTS_VMEM_LIMIT),
        name="experts",
    )(block_expert, n_valid, xs, w_gu, b_gu.reshape(e, 1, f2), w_dn, b_dn.reshape(e, 1, d))


def _combine_kernel(used_ref, dest_ref, route_ref, gate_ref, x1_ref, g2_ref, ys_ref, o_ref, buf_ref, sems):
    tile = pl.program_id(0)
    slot = tile % 2
    n_slots = buf_ref.shape[1]
    covered = route_ref.shape[0] * TOP_K

    def copies(t, s, wait):
        _chunk_copies(t, used_ref, dest_ref, buf_ref.at[s], ys_ref, sems.at[s], to_global=False, wait=wait)

    def fetch(t, s):
        buf_ref[s, covered:] = jnp.zeros((n_slots - covered, buf_ref.shape[2]), buf_ref.dtype)
        copies(t, s, False)

    @pl.when(tile == 0)
    def _():
        fetch(0, 0)

    @pl.when(tile + 1 < pl.num_programs(0))
    def _():
        fetch(tile + 1, 1 - slot)

    route = route_ref[...]
    gates = gate_ref[...]
    weights = jnp.zeros((route.shape[0], n_slots), jnp.float32)
    for k in range(TOP_K):
        weights = jnp.where(_slot_onehot(route, k, n_slots), gates[:, k:k + 1], weights)
    copies(tile, slot, True)
    moe = jnp.dot(weights.astype(jnp.bfloat16), buf_ref[slot], preferred_element_type=jnp.float32)
    o_ref[...] = x1_ref[...] + g2_ref[0] * moe


def _combine(used, dest, route, gates, x1, g2, ys, tiles_per_batch):
    t, d = x1.shape
    tm = ROUTE_TM
    grid_spec = pltpu.PrefetchScalarGridSpec(
        num_scalar_prefetch=2,
        grid=(t // tm,),
        in_specs=[pl.BlockSpec((tm, LANES), lambda i, *_: (i, 0)),
                  pl.BlockSpec((tm, LANES), lambda i, *_: (i, 0)),
                  pl.BlockSpec((tm, d), lambda i, *_: (i, 0)),
                  pl.BlockSpec((1, 1, d), lambda i, *_: (i // tiles_per_batch, 0, 0)),
                  pl.BlockSpec(memory_space=pl.ANY)],
        out_specs=pl.BlockSpec((tm, d), lambda i, *_: (i, 0)),
        scratch_shapes=[pltpu.VMEM((2, TILE_SLOTS, d), ys.dtype), pltpu.SemaphoreType.DMA((2,))],
    )
    return pl.pallas_call(
        _combine_kernel,
        out_shape=jax.ShapeDtypeStruct((t, d), jnp.float32),
        grid_spec=grid_spec,
        compiler_params=pltpu.CompilerParams(vmem_limit_bytes=VMEM_LIMIT),
        name="combine",
    )(used, dest, route, gates, x1, g2, ys)


def _score_bound(q_gain, k_gain):
    rounding = 1.02
    return HEAD_DIM * jnp.max(jnp.abs(q_gain)) * jnp.max(jnp.abs(k_gain)) * Q_SCALE * rounding

def _rope_tables(s):
    pos = jnp.arange(s, dtype=jnp.int32)
    inv = 1.0 / (ROPE_BASE ** (jnp.arange(ROPE_FREQS, dtype=jnp.float32) / ROPE_FREQS))
    ang_r = (pos // GRID_W).astype(jnp.float32)[:, None] * inv
    ang_c = (pos % GRID_W).astype(jnp.float32)[:, None] * inv
    ang = jnp.concatenate([ang_r, ang_r, ang_c, ang_c], axis=-1)
    sign = jnp.asarray(np.tile(np.repeat([-1.0, 1.0], ROPE_FREQS), 2), jnp.float32)
    cos_t = jnp.tile(jnp.cos(ang), (1, LANES // HEAD_DIM))
    sin_t = jnp.tile(jnp.sin(ang) * sign, (1, LANES // HEAD_DIM))
    return cos_t, sin_t


def kernel(x, c, ctx, c_ctx, w_ada, b_ada, g_attn, w_in, q_norm_diff, k_norm_diff, lam_q1, lam_k1, lam_q2,
           lam_k2, subln_diff, q_norm_na, k_norm_na, rpb_na, out_norm_na, w_out, g_ffn, w_router, b_router,
           w_gate_up, b_gate_up, w_down, b_down):
    depth = w_ada.shape[0]
    assert depth == 1, "single-layer kernel"
    b, s, d = x.shape
    n_ctx = ctx.shape[1]
    rows_n = s // GRID_W
    assert rows_n >= NA_ROWS and s % PROJ_TM == 0 and rows_n % NA_ROWS_PER_STEP == 0
    lam_init = 0.8 - 0.6 * math.exp(-0.3 * 0)

    rows = -(-(b + 1) // 8) * 8
    cv = jnp.zeros((rows, d), jnp.float32).at[:b].set(c).at[b].set(c_ctx)
    mod = _adaln(cv, w_ada[0], b_ada[0]).reshape(rows, 6, d)
    lat = [mod[:b, i][:, None, :] for i in range(6)]
    cxm = [jnp.broadcast_to(mod[b, i][None, None, :], (b, 1, d)) for i in range(6)]
    sh1, sc1, g1, sh2, sc2, g2 = lat

    w_in_b = w_in[0].astype(jnp.bfloat16)
    gidx = np.arange(256) // HEAD_DIM
    gsum = jnp.asarray(gidx[:, None] == gidx[None, :], jnp.bfloat16)
    tile4 = lambda v, reps: jnp.tile(v.reshape(1, -1), (1, reps))
    norms = (tile4(q_norm_diff[0], 8), tile4(k_norm_diff[0], 8), tile4(q_norm_na[0], 8), tile4(k_norm_na[0], 8))
    cos_t, sin_t = _rope_tables(s)
    g_attn2 = g_attn[0].reshape(1, d)

    qd, kd, vdt, qn, kn, vn = _proj(x, g_attn2, sh1, sc1, w_in_b, gsum, cos_t, sin_t, norms,
                                    groups=("qd", "kd", "vd", "qn", "kn", "vn"), rope=True, tm=PROJ_TM)
    kd_c, vdt_c, kn_c, vn_c = _proj(ctx, g_attn2, cxm[0], cxm[1], w_in_b, gsum, cos_t[:n_ctx], sin_t[:n_ctx],
                                    norms, groups=("kd", "vd", "kn", "vn"), rope=False, tm=n_ctx)

    lams = tuple(v[0].reshape(1, HEAD_DIM) for v in (lam_q1, lam_k1, lam_q2, lam_k2))
    diff_args = (qd, kd, kd_c, vdt, vdt_c, lams, subln_diff[0].reshape(1, LANES))
    o_diff = lax.cond(
        _score_bound(q_norm_diff[0], k_norm_diff[0]) <= EXP2_SAFE_SCORE,
        lambda a: _diff_attn(*a, lam_init=lam_init, stabilise=False),
        lambda a: _diff_attn(*a, lam_init=lam_init, stabilise=True), diff_args)
    na_args = (qn, kn, vn, kn_c, vn_c, _na_bias_table(rpb_na[0]), out_norm_na[0].reshape(1, -1))
    na_bound = _score_bound(q_norm_na[0], k_norm_na[0]) + jnp.max(jnp.abs(rpb_na[0])) * LOG2E
    o_na = lax.cond(
        na_bound <= EXP2_SAFE_SCORE,
        lambda a: _na_attn(*a, stabilise=False),
        lambda a: _na_attn(*a, stabilise=True), na_args)

    x1, h2, route, gates, counts = _out_route(o_diff, o_na, x, w_out[0].astype(jnp.bfloat16), g1,
                                              g_ffn[0].reshape(1, d), sh2, sc2, w_router[0], b_router[0])

    t = b * s
    n_exp = w_router.shape[2]
    assert n_exp == N_EXPERTS
    tile_cnt = counts[:, 0, :n_exp].astype(jnp.int32)
    tile_run = (tile_cnt + RUN_ALIGN - 1) // RUN_ALIGN * RUN_ALIGN
    totals = jnp.sum(tile_run, axis=0)
    padded = (totals + EXPERT_TM - 1) // EXPERT_TM * EXPERT_TM
    pad_ends = jnp.cumsum(padded)
    pad_starts = pad_ends - padded
    tile_loc = jnp.cumsum(tile_run, axis=1) - tile_run
    tile_glob = pad_starts[None, :] + jnp.cumsum(tile_run, axis=0) - tile_run
    max_rows = t * TOP_K + tile_cnt.size * (RUN_ALIGN - 1)
    n_tiles = -(-max_rows // EXPERT_TM) + n_exp
    tile_row0 = jnp.arange(n_tiles, dtype=jnp.int32) * EXPERT_TM
    block_expert = jnp.minimum(jnp.sum(tile_row0[:, None] >= pad_ends[None, :], axis=1), n_exp - 1).astype(jnp.int32)
    n_valid = (pad_ends[-1:] // EXPERT_TM).astype(jnp.int32)
    run_end = tile_loc + tile_run
    chunk_row = (jnp.arange(TILE_SLOTS // RUN_ALIGN, dtype=jnp.int32) * RUN_ALIGN)[None, :, None]
    in_run = (chunk_row >= tile_loc[:, None, :]) & (chunk_row < run_end[:, None, :])
    chunk_dest = jnp.sum(jnp.where(in_run, (tile_glob - tile_loc)[:, None, :], 0), axis=-1) + chunk_row[:, :, 0]
    chunks_used = run_end[:, -1] // RUN_ALIGN
    moves = (chunks_used.astype(jnp.int32), chunk_dest.reshape(-1).astype(jnp.int32))
    pad_row = (pad_starts + totals).astype(jnp.int32)
    pad_chunks = ((padded - totals) // RUN_ALIGN).astype(jnp.int32)

    xs = _dispatch(*moves, pad_row, pad_chunks, n_valid, route, h2, n_tiles * EXPERT_TM)
    ys = _experts(block_expert, n_valid, xs, w_gate_up[0], b_gate_up[0], w_down[0], b_down[0])
    out = _combine(*moves, route, gates, x1, g2, ys, s // ROUTE_TM)
    return out.reshape(b, s, d)
```

```python
import functools
import math

import jax
import jax.numpy as jnp
import numpy as np
from jax import lax
from jax.experimental import pallas as pl
from jax.experimental.pallas import tpu as pltpu

GRID_W = 64
HEAD_DIM = 64
NA_ROWS = 8
NA_COLS = 16
ROPE_BASE = 10000.0
ROPE_FREQS = HEAD_DIM // 4
N_EXPERTS = 32
TOP_K = 4
SWIGLU_ALPHA = 1.702
SWIGLU_LIMIT = 7.0
NORM_EPS = 1e-6

LANES = 128
NEG = -1e30
LOG2E = math.log2(math.e)
Q_SCALE = HEAD_DIM ** -0.5 * LOG2E
EXP2_SAFE_SCORE = 50.0

PROJ_TM = 512
DIFF_TQ = 1024
DIFF_STATIC_KLOOP = True
DIFF_SCORE_BUFS = 4
NA_ROWS_PER_STEP = 32
NA_ROW_UNROLL = 32
ROUTE_TM = 512
EXPERT_TM = 512
RUN_ALIGN = 16
TILE_SLOTS = ROUTE_TM * TOP_K + N_EXPERTS * RUN_ALIGN
CHUNK_UNROLL = 8
VMEM_LIMIT = 48 * 1024 * 1024
EXPERTS_VMEM_LIMIT = 56 * 1024 * 1024

_NT = (((1,), (1,)), ((), ()))


def _f32(x):
    return x.astype(jnp.float32)


def _adaln_kernel(cv_ref, w_ref, b_ref, o_ref):
    cv = cv_ref[...]
    act = cv * (1.0 / (1.0 + jnp.exp(-cv)))
    o_ref[...] = jnp.dot(act, w_ref[...], precision=lax.Precision.HIGHEST,
                         preferred_element_type=jnp.float32) + b_ref[...]


def _adaln(cv, w_ada, b_ada):
    rows, d = cv.shape
    n = w_ada.shape[1]
    tn = 1024
    return pl.pallas_call(
        _adaln_kernel,
        out_shape=jax.ShapeDtypeStruct((rows, n), jnp.float32),
        grid=(n // tn,),
        in_specs=[pl.BlockSpec((rows, d), lambda j: (0, 0)),
                  pl.BlockSpec((d, tn), lambda j: (0, j)),
                  pl.BlockSpec((1, tn), lambda j: (0, j))],
        out_specs=pl.BlockSpec((rows, tn), lambda j: (0, j)),
        compiler_params=pltpu.CompilerParams(vmem_limit_bytes=VMEM_LIMIT),
        name="adaln",
    )(cv, w_ada, b_ada.reshape(1, n))


def _proj_kernel(*refs, groups, rope, d_model):
    (x_ref, g_ref, sh_ref, sc_ref, w_ref, gsum_ref, cos_ref, sin_ref,
     qnd_ref, knd_ref, qnn_ref, knn_ref) = refs[:12]
    outs = dict(zip(groups, refs[12:]))
    width = 4 * LANES

    x = x_ref[0]
    y = x * lax.rsqrt(jnp.mean(x * x, axis=-1, keepdims=True) + NORM_EPS) * g_ref[...]
    h = (y * (1.0 + sc_ref[0]) + sh_ref[0]).astype(jnp.bfloat16)

    group_col = {"qd": 0, "kd": 1, "vd": 2, "qn": 3, "kn": 4, "vn": 5}
    gains = {"qd": qnd_ref[...] * Q_SCALE, "kd": knd_ref[...],
             "qn": qnn_ref[...] * Q_SCALE, "kn": knn_ref[...]}
    lane = lax.broadcasted_iota(jnp.int32, (x.shape[0], LANES), 1)
    first_half = (lane % (2 * ROPE_FREQS)) < ROPE_FREQS

    for name in groups:
        c0 = group_col[name] * width
        p = jnp.dot(h, w_ref[:, c0:c0 + width], preferred_element_type=jnp.float32)
        if name in gains:
            blocks = []
            for j in range(width // 256):
                blk = p[:, j * 256:(j + 1) * 256]
                ss = jnp.dot((blk * blk).astype(jnp.bfloat16), gsum_ref[...],
                             preferred_element_type=jnp.float32)
                blocks.append(blk * lax.rsqrt(ss * (1.0 / HEAD_DIM) + NORM_EPS))
            p = jnp.concatenate(blocks, axis=1) * gains[name]
        if rope and name in ("qd", "kd"):
            blocks = []
            for j in range(width // LANES):
                blk = p[:, j * LANES:(j + 1) * LANES]
                partner = jnp.where(first_half,
                                    pltpu.roll(blk, LANES - ROPE_FREQS, axis=1),
                                    pltpu.roll(blk, ROPE_FREQS, axis=1))
                blocks.append(blk * cos_ref[...] + partner * sin_ref[...])
            p = jnp.concatenate(blocks, axis=1)
        if name == "vd":
            pt = p.T.reshape(width // LANES, LANES, p.shape[0])
            outs[name][0, :, 0] = pt.astype(jnp.bfloat16)
        else:
            outs[name][0] = p.astype(jnp.bfloat16)


def _proj(x, g_attn, shift, scale, w_in, gsum, cos_t, sin_t, norms, *, groups, rope, tm):
    b, s, d = x.shape
    nt = s // tm
    width = 4 * LANES
    out_shape, out_specs = [], []
    for name in groups:
        if name == "vd":
            out_shape.append(jax.ShapeDtypeStruct((b, 4, nt, LANES, tm), jnp.bfloat16))
            out_specs.append(pl.BlockSpec((1, 4, 1, LANES, tm), lambda bi, i: (bi, 0, i, 0, 0)))
        else:
            out_shape.append(jax.ShapeDtypeStruct((b, s, width), jnp.bfloat16))
            out_specs.append(pl.BlockSpec((1, tm, width), lambda bi, i: (bi, i, 0)))
    row = lambda bi, i: (0, 0)
    per_b = lambda bi, i: (bi, 0, 0)
    kern = functools.partial(_proj_kernel, groups=groups, rope=rope, d_model=d)
    return pl.pallas_call(
        kern,
        out_shape=out_shape,
        grid=(b, nt),
        in_specs=[pl.BlockSpec((1, tm, d), lambda bi, i: (bi, i, 0)),
                  pl.BlockSpec((1, d), row),
                  pl.BlockSpec((1, 1, d), per_b),
                  pl.BlockSpec((1, 1, d), per_b),
                  pl.BlockSpec(w_in.shape, row),
                  pl.BlockSpec(gsum.shape, row),
                  pl.BlockSpec((tm, LANES), lambda bi, i: (i, 0)),
                  pl.BlockSpec((tm, LANES), lambda bi, i: (i, 0)),
                  pl.BlockSpec((1, width), row), pl.BlockSpec((1, width), row),
                  pl.BlockSpec((1, width), row), pl.BlockSpec((1, width), row)],
        out_specs=out_specs,
        compiler_params=pltpu.CompilerParams(vmem_limit_bytes=VMEM_LIMIT),
        name="proj_rope" if rope else "proj_ctx",
    )(x, g_attn, shift, scale, w_in, gsum, cos_t, sin_t, *norms)


def _diff_attn_kernel(q_ref, k_ref, kc_ref, vt_ref, vtc_ref, lq1_ref, lk1_ref, lq2_ref, lk2_ref,
                      subln_ref, o_ref, m_ref, l_ref, acc_ref, *s_refs, lam_init, n_kblocks, stabilise):
    q = q_ref[0]
    lane = lax.broadcasted_iota(jnp.int32, q.shape, 1)
    zero = jnp.zeros_like(q)
    qz = (jnp.where(lane < HEAD_DIM, q, zero), jnp.where(lane >= HEAD_DIM, q, zero))

    l_ref[...] = jnp.zeros(l_ref.shape, jnp.float32)
    acc_ref[...] = jnp.zeros(acc_ref.shape, jnp.float32)

    def sum8(p):
        return jnp.sum(p.reshape(p.shape[0] // 8, 8, p.shape[1]), axis=0)

    def scores(k, mp):
        return lax.dot_general(k, qz[mp], _NT, preferred_element_type=jnp.float32)

    if stabilise:
        m_ref[...] = jnp.full(m_ref.shape, NEG, jnp.float32)

        def block(k, vt):
            for mp in range(2):
                s = scores(k, mp)
                m_old = m_ref[mp]
                m_new = jnp.maximum(m_old, jnp.max(s, axis=0, keepdims=True))
                alpha = jnp.exp2(m_old - m_new)
                p = jnp.exp2(s - m_new)
                l_ref[mp] = alpha * l_ref[mp] + sum8(p)
                acc_ref[mp] = alpha * acc_ref[mp] + jnp.dot(vt, p.astype(jnp.bfloat16),
                                                            preferred_element_type=jnp.float32)
                m_ref[mp] = m_new

        def body(i, carry):
            block(k_ref[0, i], vt_ref[0, 0, i])
            return carry

        lax.fori_loop(0, n_kblocks, body, 0)
        block(kc_ref[0], vtc_ref[0, 0, 0])
    else:
        n_bufs = len(s_refs)
        ahead = n_bufs // 2

        def produce_into(k, s_ref):
            for mp in range(2):
                s_ref[mp, :k.shape[0]] = scores(k, mp)

        def consume(s_ref, vt):
            for mp in range(2):
                p = jnp.exp2(s_ref[mp, :vt.shape[1]])
                l_ref[mp] += sum8(p)
                acc_ref[mp] += jnp.dot(vt, p.astype(jnp.bfloat16), preferred_element_type=jnp.float32)

        for i in range(ahead):
            produce_into(k_ref[0, i], s_refs[i])

        def body(j, carry):
            base = n_bufs * j
            for u in range(n_bufs):
                produce_into(k_ref[0, base + u + ahead], s_refs[(u + ahead) % n_bufs])
                consume(s_refs[u], vt_ref[0, 0, base + u])
            return carry

        n_main = 0 if DIFF_STATIC_KLOOP else (n_kblocks - ahead) // n_bufs
        lax.fori_loop(0, n_main, body, 0)
        for i in range(n_main * n_bufs, n_kblocks + 1):
            nxt = i + ahead
            if nxt <= n_kblocks:
                produce_into(kc_ref[0] if nxt == n_kblocks else k_ref[0, nxt], s_refs[nxt % n_bufs])
            consume(s_refs[i % n_bufs], vtc_ref[0, 0, 0] if i == n_kblocks else vt_ref[0, 0, i])

    lam = (jnp.exp(jnp.sum(lq1_ref[...] * lk1_ref[...], keepdims=True))
           - jnp.exp(jnp.sum(lq2_ref[...] * lk2_ref[...], keepdims=True)) + lam_init)
    l1 = jnp.sum(l_ref[0], axis=0, keepdims=True)
    l2 = jnp.sum(l_ref[1], axis=0, keepdims=True)
    o = acc_ref[0] / l1 - lam * (acc_ref[1] / l2)
    ot = o.T
    ot = ot * lax.rsqrt(jnp.mean(ot * ot, axis=-1, keepdims=True) + NORM_EPS)
    o_ref[0] = (ot * subln_ref[...] * (1.0 - lam_init)).astype(o_ref.dtype)


def _diff_attn(qd, kd, kd_c, vdt, vdt_c, lams, subln, *, lam_init, stabilise):
    b, s, width = qd.shape
    heads = width // LANES
    tk = vdt.shape[-1]
    nkb = s // tk
    c = kd_c.shape[1]
    kd4 = kd.reshape(b, nkb, tk, width)
    tq = DIFF_TQ
    assert nkb >= DIFF_SCORE_BUFS // 2 and c <= tk
    kern = functools.partial(_diff_attn_kernel, lam_init=lam_init, n_kblocks=nkb, stabilise=stabilise)
    vec = pl.BlockSpec((1, HEAD_DIM), lambda bi, h, i: (0, 0))
    return pl.pallas_call(
        kern,
        out_shape=jax.ShapeDtypeStruct((b, s, width), jnp.bfloat16),
        grid=(b, heads, s // tq),
        in_specs=[pl.BlockSpec((1, tq, LANES), lambda bi, h, i: (bi, i, h)),
                  pl.BlockSpec((1, nkb, tk, LANES), lambda bi, h, i: (bi, 0, 0, h)),
                  pl.BlockSpec((1, c, LANES), lambda bi, h, i: (bi, 0, h)),
                  pl.BlockSpec((1, 1, nkb, LANES, tk), lambda bi, h, i: (bi, h, 0, 0, 0)),
                  pl.BlockSpec((1, 1, 1, LANES, c), lambda bi, h, i: (bi, h, 0, 0, 0)),
                  vec, vec, vec, vec,
                  pl.BlockSpec((1, LANES), lambda bi, h, i: (0, 0))],
        out_specs=pl.BlockSpec((1, tq, LANES), lambda bi, h, i: (bi, i, h)),
        scratch_shapes=[pltpu.VMEM((2, 1, tq), jnp.float32),
                        pltpu.VMEM((2, 8, tq), jnp.float32),
                        pltpu.VMEM((2, LANES, tq), jnp.float32)]
                       + [pltpu.VMEM((2, tk, tq), jnp.float32)] * DIFF_SCORE_BUFS,
        compiler_params=pltpu.CompilerParams(vmem_limit_bytes=VMEM_LIMIT),
        name="diff_attn",
    )(qd, kd4, kd_c, vdt, vdt_c, *lams, subln)


def _na_bias_table(rpb):
    heads = rpb.shape[0]
    r = np.arange(2 * NA_ROWS - 1)[:, None, None]
    row_sel = (r == np.arange(NA_ROWS)[None, None, :] - np.arange(NA_ROWS)[None, :, None] + (NA_ROWS - 1))
    wq = np.arange(GRID_W)[:, None]
    wk = np.arange(GRID_W)[None, :]
    start = np.clip(wq - NA_COLS // 2, 0, GRID_W - NA_COLS)
    inside = (wk >= start) & (wk < start + NA_COLS)
    col_sel = (np.arange(2 * NA_COLS - 1)[:, None, None] == (wk - wq + (NA_COLS - 1))[None]) & inside[None]
    t = jnp.einsum("hrc,rdi,cqk->dhqik", rpb, jnp.asarray(row_sel, jnp.float32), jnp.asarray(col_sel, jnp.float32),
                   precision=lax.Precision.HIGHEST)
    t = jnp.where(inside[None, None, :, None, :], t * LOG2E, NEG)
    return t.reshape(NA_ROWS, heads // 2, 2 * GRID_W, NA_ROWS * GRID_W).astype(jnp.float32)


def _na_attn_kernel(q_ref, k_ref, v_ref, kc_ref, vc_ref, bias_ref, gain_ref, o_ref, octx_ref, lctx_ref, *,
                    rows_n, rows_per_step, stabilise):
    step = pl.program_id(2)
    win = NA_ROWS * GRID_W
    low = lax.broadcasted_iota(jnp.int32, (GRID_W, LANES), 1) < HEAD_DIM
    kc = kc_ref[0]
    vc = vc_ref[0]
    gain = gain_ref[...]

    def head_lanes(q, hh):
        keep = (lax.broadcasted_iota(jnp.int32, q.shape, 1) < HEAD_DIM) == (hh == 0)
        return jnp.where(keep, q, jnp.zeros_like(q))

    if not stabilise:
        q_all = q_ref[0]
        for hh in range(2):
            p = jnp.exp2(lax.dot_general(head_lanes(q_all, hh), kc, _NT, preferred_element_type=jnp.float32))
            lctx_ref[hh] = jnp.broadcast_to(jnp.sum(p, axis=-1, keepdims=True), lctx_ref.shape[1:])
            octx_ref[hh] = jnp.dot(p.astype(jnp.bfloat16), vc, preferred_element_type=jnp.float32)

    def row(j):
        r = step * rows_per_step + j
        r0 = jnp.clip(r - NA_ROWS // 2, 0, rows_n - NA_ROWS)
        d = r - r0
        start = pl.multiple_of(r0 * GRID_W, GRID_W)
        rows = pl.ds(pl.multiple_of(j * GRID_W, GRID_W), GRID_W)
        qrow = q_ref[0, rows, :]
        kw = k_ref[0, pl.ds(start, win), :]
        vw = v_ref[0, pl.ds(start, win), :]
        q2 = jnp.concatenate([head_lanes(qrow, 0), head_lanes(qrow, 1)], axis=0)
        s = lax.dot_general(q2, kw, _NT, preferred_element_type=jnp.float32) + bias_ref[d, 0]
        if stabilise:
            s_ctx = lax.dot_general(q2, kc, _NT, preferred_element_type=jnp.float32)
            m = jnp.maximum(jnp.max(s, axis=-1, keepdims=True), jnp.max(s_ctx, axis=-1, keepdims=True))
            p = jnp.exp2(s - m)
            p_ctx = jnp.exp2(s_ctx - m)
            l = jnp.sum(p, axis=-1, keepdims=True) + jnp.sum(p_ctx, axis=-1, keepdims=True)
            o2 = (jnp.dot(p.astype(jnp.bfloat16), vw, preferred_element_type=jnp.float32)
                  + jnp.dot(p_ctx.astype(jnp.bfloat16), vc, preferred_element_type=jnp.float32)) / l
            o = jnp.where(low, o2[:GRID_W], o2[GRID_W:])
        else:
            p = jnp.exp2(s)
            l = jnp.sum(p, axis=-1, keepdims=True)
            o2 = jnp.dot(p.astype(jnp.bfloat16), vw, preferred_element_type=jnp.float32)
            o = jnp.where(low,
                          (o2[:GRID_W] + octx_ref[0, rows, :]) / (l[:GRID_W] + lctx_ref[0, rows, :]),
                          (o2[GRID_W:] + octx_ref[1, rows, :]) / (l[GRID_W:] + lctx_ref[1, rows, :]))
        sq = o * o
        ms0 = jnp.sum(jnp.where(low, sq, 0.0), axis=-1, keepdims=True) * (1.0 / HEAD_DIM)
        ms1 = jnp.sum(jnp.where(low, 0.0, sq), axis=-1, keepdims=True) * (1.0 / HEAD_DIM)
        inv = jnp.where(low, lax.rsqrt(ms0 + NORM_EPS), lax.rsqrt(ms1 + NORM_EPS))
        o_ref[0, rows, :] = (o * inv * gain).astype(o_ref.dtype)

    def row_group(jj, carry):
        for u in range(NA_ROW_UNROLL):
            row(NA_ROW_UNROLL * jj + u)
        return carry

    lax.fori_loop(0, rows_per_step // NA_ROW_UNROLL, row_group, 0)


def _na_attn(qn, kn, vn, kn_c, vn_c, bias, out_gain, *, stabilise):
    b, s, width = qn.shape
    pairs = width // LANES
    rows_n = s // GRID_W
    rps = NA_ROWS_PER_STEP
    c = kn_c.shape[1]
    tq = rps * GRID_W
    kern = functools.partial(_na_attn_kernel, rows_n=rows_n, rows_per_step=rps, stabilise=stabilise)
    whole = lambda bi, h, i: (bi, 0, h)
    return pl.pallas_call(
        kern,
        out_shape=jax.ShapeDtypeStruct((b, s, width), jnp.bfloat16),
        grid=(b, pairs, rows_n // rps),
        in_specs=[pl.BlockSpec((1, tq, LANES), lambda bi, h, i: (bi, i, h)),
                  pl.BlockSpec((1, s, LANES), whole),
                  pl.BlockSpec((1, s, LANES), whole),
                  pl.BlockSpec((1, c, LANES), whole),
                  pl.BlockSpec((1, c, LANES), whole),
                  pl.BlockSpec((NA_ROWS, 1, 2 * GRID_W, NA_ROWS * GRID_W), lambda bi, h, i: (0, h, 0, 0)),
                  pl.BlockSpec((1, LANES), lambda bi, h, i: (0, h))],
        out_specs=pl.BlockSpec((1, tq, LANES), lambda bi, h, i: (bi, i, h)),
        scratch_shapes=[pltpu.VMEM((2, tq, LANES), jnp.float32), pltpu.VMEM((2, tq, LANES), jnp.float32)],
        compiler_params=pltpu.CompilerParams(vmem_limit_bytes=VMEM_LIMIT),
        name="na_attn",
    )(qn, kn, vn, kn_c, vn_c, bias, out_gain)


def _out_route_kernel(od_ref, on_ref, x_ref, w_ref, g1_ref, gf_ref, sh_ref, sc_ref, wr_ref, br_ref,
                      x1_ref, h2_ref, route_ref, gate_ref, cnt_ref):
    half = od_ref.shape[2]
    attn = (jnp.dot(od_ref[0], w_ref[:half], preferred_element_type=jnp.float32)
            + jnp.dot(on_ref[0], w_ref[half:], preferred_element_type=jnp.float32))
    x1 = x_ref[0] + g1_ref[0] * attn
    x1_ref[...] = x1
    y = x1 * lax.rsqrt(jnp.mean(x1 * x1, axis=-1, keepdims=True) + NORM_EPS) * gf_ref[...]
    h2 = y * (1.0 + sc_ref[0]) + sh_ref[0]
    h2_hi = h2.astype(jnp.bfloat16)
    h2_ref[...] = h2_hi

    h2_lo = (h2 - h2_hi.astype(jnp.float32)).astype(jnp.bfloat16)
    wr = wr_ref[...]
    wr_hi = wr.astype(jnp.bfloat16)
    wr_lo = (wr - wr_hi.astype(jnp.float32)).astype(jnp.bfloat16)
    logits = (jnp.dot(h2_hi, wr_hi, preferred_element_type=jnp.float32)
              + jnp.dot(h2_lo, wr_hi, preferred_element_type=jnp.float32)
              + jnp.dot(h2_hi, wr_lo, preferred_element_type=jnp.float32)) + br_ref[...]
    tm = logits.shape[0]
    lane = lax.broadcasted_iota(jnp.int32, logits.shape, 1)
    lane_f = lane.astype(jnp.float32)
    vals, idxs = [], []
    cur = logits
    for _ in range(TOP_K):
        mx = jnp.max(cur, axis=-1, keepdims=True)
        ik = jnp.min(jnp.where(cur == mx, lane_f, float(LANES)), axis=-1, keepdims=True)
        vals.append(mx)
        idxs.append(ik)
        cur = jnp.where(lane_f == ik, -3e38, cur)
    exps = [jnp.exp(v - vals[0]) for v in vals]
    denom = exps[0] + exps[1] + exps[2] + exps[3]

    onehot = jnp.zeros(logits.shape, jnp.float32)
    for ik in idxs:
        onehot = onehot + jnp.where(lane_f == ik, 1.0, 0.0)
    ti = lax.broadcasted_iota(jnp.int32, (tm, tm), 0)
    tj = lax.broadcasted_iota(jnp.int32, (tm, tm), 1)
    lower = jnp.where(tj < ti, 1.0, 0.0).astype(jnp.bfloat16)
    prefix = jnp.dot(lower, onehot.astype(jnp.bfloat16), preferred_element_type=jnp.float32)
    counts = jnp.sum(onehot, axis=0, keepdims=True)
    run = jnp.ceil(counts * (1.0 / RUN_ALIGN)) * RUN_ALIGN
    ei = lax.broadcasted_iota(jnp.int32, (LANES, LANES), 0)
    ej = lax.broadcasted_iota(jnp.int32, (LANES, LANES), 1)
    before = jnp.where(ei < ej, 1.0, 0.0)
    starts = jnp.dot(jnp.broadcast_to(run, (8, LANES)), before, precision=lax.Precision.HIGHEST,
                     preferred_element_type=jnp.float32)[0:1]
    slot = prefix + starts
    route = jnp.zeros(logits.shape, jnp.float32)
    gates = jnp.zeros(logits.shape, jnp.float32)
    for k in range(TOP_K):
        pos = jnp.sum(jnp.where(lane_f == idxs[k], slot, 0.0), axis=-1, keepdims=True)
        route = jnp.where(lane == k, idxs[k], route)
        route = jnp.where(lane == TOP_K + k, pos, route)
        gates = jnp.where(lane == k, exps[k] / denom, gates)
    route_ref[...] = route.astype(jnp.int32)
    gate_ref[...] = gates
    cnt_ref[0] = counts


def _out_route(o_diff, o_na, x, w_out, g1, g_ffn, sh2, sc2, w_router, b_router):
    b, s, d = x.shape
    tm = ROUTE_TM
    half = o_diff.shape[2]
    e = w_router.shape[1]
    wr = jnp.zeros((d, LANES), jnp.float32).at[:, :e].set(w_router)
    br = jnp.full((1, LANES), NEG, jnp.float32).at[0, :e].set(b_router)
    tile = lambda bi, i: (bi, i, 0)
    flat = lambda bi, i: (bi * (s // tm) + i, 0)
    row = lambda bi, i: (0, 0)
    per_b = lambda bi, i: (bi, 0, 0)
    return pl.pallas_call(
        _out_route_kernel,
        out_shape=[jax.ShapeDtypeStruct((b * s, d), jnp.float32),
                   jax.ShapeDtypeStruct((b * s, d), jnp.bfloat16),
                   jax.ShapeDtypeStruct((b * s, LANES), jnp.int32),
                   jax.ShapeDtypeStruct((b * s, LANES), jnp.float32),
                   jax.ShapeDtypeStruct((b * (s // tm), 1, LANES), jnp.float32)],
        grid=(b, s // tm),
        in_specs=[pl.BlockSpec((1, tm, half), tile),
                  pl.BlockSpec((1, tm, half), tile),
                  pl.BlockSpec((1, tm, d), tile),
                  pl.BlockSpec(w_out.shape, row),
                  pl.BlockSpec((1, 1, d), per_b),
                  pl.BlockSpec((1, d), row),
                  pl.BlockSpec((1, 1, d), per_b),
                  pl.BlockSpec((1, 1, d), per_b),
                  pl.BlockSpec((d, LANES), row),
                  pl.BlockSpec((1, LANES), row)],
        out_specs=[pl.BlockSpec((tm, d), flat),
                   pl.BlockSpec((tm, d), flat),
                   pl.BlockSpec((tm, LANES), flat),
                   pl.BlockSpec((tm, LANES), flat),
                   pl.BlockSpec((1, 1, LANES), lambda bi, i: (bi * (s // tm) + i, 0, 0))],
        compiler_params=pltpu.CompilerParams(vmem_limit_bytes=VMEM_LIMIT),
        name="out_route",
    )(o_diff, o_na, x, w_out, g1, g_ffn, sh2, sc2, wr, br)


def _chunk_copies(tile, used_ref, dest_ref, local_ref, global_ref, sem, *, to_global, wait):
    n_chunks = local_ref.shape[0] // RUN_ALIGN

    def one(c, carry):
        lref = local_ref.at[pl.ds(pl.multiple_of(c * RUN_ALIGN, RUN_ALIGN), RUN_ALIGN), :]
        gref = global_ref.at[pl.ds(pl.multiple_of(dest_ref[tile * n_chunks + c], RUN_ALIGN), RUN_ALIGN), :]
        cp = pltpu.make_async_copy(lref, gref, sem) if to_global else pltpu.make_async_copy(gref, lref, sem)
        cp.wait() if wait else cp.start()
        return carry

    def group(g, carry):
        for u in range(CHUNK_UNROLL):
            one(g * CHUNK_UNROLL + u, carry)
        return carry

    used = used_ref[tile]
    n_groups = used // CHUNK_UNROLL
    lax.fori_loop(0, n_groups, group, 0)
    lax.fori_loop(n_groups * CHUNK_UNROLL, used, one, 0)


def _slot_onehot(route, k, n_slots):
    slots = lax.broadcasted_iota(jnp.int32, (route.shape[0], n_slots), 1)
    return slots == route[:, TOP_K + k:TOP_K + k + 1]


def _zero_unsorted_rows(pad_row_ref, pad_chunks_ref, nv_ref, zero_ref, xs_ref, sem):
    zero_ref[...] = jnp.zeros(zero_ref.shape, zero_ref.dtype)
    n_tiles = xs_ref.shape[0] // EXPERT_TM
    for wait in (False, True):
        def pad_chunk(e, j, c):
            row = pl.multiple_of(pad_row_ref[e] + j * RUN_ALIGN, RUN_ALIGN)
            cp = pltpu.make_async_copy(zero_ref.at[pl.ds(0, RUN_ALIGN), :], xs_ref.at[pl.ds(row, RUN_ALIGN), :], sem)
            cp.wait() if wait else cp.start()
            return c

        def unused_tile(i, c):
            row = pl.multiple_of(i * EXPERT_TM, EXPERT_TM)
            cp = pltpu.make_async_copy(zero_ref, xs_ref.at[pl.ds(row, EXPERT_TM), :], sem)
            cp.wait() if wait else cp.start()
            return c

        lax.fori_loop(0, N_EXPERTS, lambda e, c: lax.fori_loop(
            0, pad_chunks_ref[e], functools.partial(pad_chunk, e), c), 0)
        lax.fori_loop(nv_ref[0], n_tiles, unused_tile, 0)


def _dispatch_kernel(used_ref, dest_ref, pad_row_ref, pad_chunks_ref, nv_ref, route_ref, h_ref, xs_ref,
                     buf_ref, zero_ref, sems, zero_sem):
    tile = pl.program_id(0)
    slot = tile % 2

    @pl.when(tile == 0)
    def _():
        _zero_unsorted_rows(pad_row_ref, pad_chunks_ref, nv_ref, zero_ref, xs_ref, zero_sem)

    route = route_ref[...]
    n_slots = buf_ref.shape[1]
    sel = _slot_onehot(route, 0, n_slots)
    for k in range(1, TOP_K):
        sel = sel | _slot_onehot(route, k, n_slots)
    perm = jnp.where(sel, 1.0, 0.0).astype(jnp.bfloat16)
    rows = lax.dot_general(perm, h_ref[...], (((0,), (0,)), ((), ())),
                           preferred_element_type=jnp.float32)
    buf_ref[slot] = rows.astype(buf_ref.dtype)

    def copies(t, s, wait):
        _chunk_copies(t, used_ref, dest_ref, buf_ref.at[s], xs_ref, sems.at[s], to_global=True, wait=wait)

    copies(tile, slot, False)

    @pl.when(tile > 0)
    def _():
        copies(tile - 1, 1 - slot, True)

    @pl.when(tile == pl.num_programs(0) - 1)
    def _():
        copies(tile, slot, True)


def _dispatch(used, dest, pad_row, pad_chunks, n_valid, route, h2, n_rows):
    t, d = h2.shape
    tm = ROUTE_TM
    grid_spec = pltpu.PrefetchScalarGridSpec(
        num_scalar_prefetch=5,
        grid=(t // tm,),
        in_specs=[pl.BlockSpec((tm, LANES), lambda i, *_: (i, 0)),
                  pl.BlockSpec((tm, d), lambda i, *_: (i, 0))],
        out_specs=pl.BlockSpec(memory_space=pl.ANY),
        scratch_shapes=[pltpu.VMEM((2, TILE_SLOTS, d), h2.dtype), pltpu.VMEM((EXPERT_TM, d), h2.dtype),
                        pltpu.SemaphoreType.DMA((2,)), pltpu.SemaphoreType.DMA(())],
    )
    return pl.pallas_call(
        _dispatch_kernel,
        out_shape=jax.ShapeDtypeStruct((n_rows, d), h2.dtype),
        grid_spec=grid_spec,
        compiler_params=pltpu.CompilerParams(vmem_limit_bytes=VMEM_LIMIT),
        name="dispatch",
    )(used, dest, pad_row, pad_chunks, n_valid, route, h2)


def _experts_kernel(be_ref, nv_ref, xs_ref, wgu_ref, bgu_ref, wdn_ref, bdn_ref, ys_ref, wgu_b_ref, wdn_b_ref):
    i = pl.program_id(0)
    valid = i < nv_ref[0]
    new_expert = (i == 0) | (be_ref[i] != be_ref[jnp.maximum(i - 1, 0)])

    @pl.when(valid & new_expert)
    def _():
        wgu_b_ref[...] = wgu_ref[0].astype(jnp.bfloat16)
        wdn_b_ref[...] = wdn_ref[0].astype(jnp.bfloat16)

    @pl.when(valid)
    def _():
        gu = jnp.dot(xs_ref[...], wgu_b_ref[...], preferred_element_type=jnp.float32) + bgu_ref[0]
        f = gu.shape[1] // 2
        glu = jnp.minimum(gu[:, :f], SWIGLU_LIMIT)
        lin = jnp.clip(gu[:, f:], -SWIGLU_LIMIT, SWIGLU_LIMIT)
        act = glu * (1.0 / (1.0 + jnp.exp(-SWIGLU_ALPHA * glu))) * (lin + 1.0)
        y = jnp.dot(act.astype(jnp.bfloat16), wdn_b_ref[...], preferred_element_type=jnp.float32) + bdn_ref[0]
        ys_ref[...] = y.astype(ys_ref.dtype)

    @pl.when(jnp.logical_not(valid))
    def _():
        ys_ref[...] = jnp.zeros(ys_ref.shape, ys_ref.dtype)


def _experts(block_expert, n_valid, xs, w_gu, b_gu, w_dn, b_dn):
    n_rows, dw = xs.shape
    tm = EXPERT_TM
    e, d, f2 = w_gu.shape
    grid_spec = pltpu.PrefetchScalarGridSpec(
        num_scalar_prefetch=2,
        grid=(n_rows // tm,),
        in_specs=[pl.BlockSpec((tm, dw), lambda i, be, nv: (jnp.minimum(i, nv[0] - 1), 0)),
                  pl.BlockSpec((1, d, f2), lambda i, be, nv: (be[i], 0, 0)),
                  pl.BlockSpec((1, 1, f2), lambda i, be, nv: (be[i], 0, 0)),
                  pl.BlockSpec((1, f2 // 2, d), lambda i, be, nv: (be[i], 0, 0)),
                  pl.BlockSpec((1, 1, d), lambda i, be, nv: (be[i], 0, 0))],
        out_specs=pl.BlockSpec((tm, d), lambda i, be, nv: (i, 0)),
        scratch_shapes=[pltpu.VMEM((d, f2), jnp.bfloat16), pltpu.VMEM((f2 // 2, d), jnp.bfloat16)],
    )
    return pl.pallas_call(
        _experts_kernel,
        out_shape=jax.ShapeDtypeStruct((n_rows, d), jnp.bfloat16),
        grid_spec=grid_spec,
        compiler_params=pltpu.CompilerParams(vmem_limit_bytes=EXPERTS_VMEM_LIMIT),
        name="experts",
    )(block_expert, n_valid, xs, w_gu, b_gu.reshape(e, 1, f2), w_dn, b_dn.reshape(e, 1, d))


def _combine_kernel(used_ref, dest_ref, route_ref, gate_ref, x1_ref, g2_ref, ys_ref, o_ref, buf_ref, sems):
    tile = pl.program_id(0)
    slot = tile % 2
    n_slots = buf_ref.shape[1]
    covered = route_ref.shape[0] * TOP_K

    def copies(t, s, wait):
        _chunk_copies(t, used_ref, dest_ref, buf_ref.at[s], ys_ref, sems.at[s], to_global=False, wait=wait)

    def fetch(t, s):
        buf_ref[s, covered:] = jnp.zeros((n_slots - covered, buf_ref.shape[2]), buf_ref.dtype)
        copies(t, s, False)

    @pl.when(tile == 0)
    def _():
        fetch(0, 0)

    @pl.when(tile + 1 < pl.num_programs(0))
    def _():
        fetch(tile + 1, 1 - slot)

    route = route_ref[...]
    gates = gate_ref[...]
    weights = jnp.zeros((route.shape[0], n_slots), jnp.float32)
    for k in range(TOP_K):
        weights = jnp.where(_slot_onehot(route, k, n_slots), gates[:, k:k + 1], weights)
    copies(tile, slot, True)
    moe = jnp.dot(weights.astype(jnp.bfloat16), buf_ref[slot], preferred_element_type=jnp.float32)
    o_ref[...] = x1_ref[...] + g2_ref[0] * moe


def _combine(used, dest, route, gates, x1, g2, ys, tiles_per_batch):
    t, d = x1.shape
    tm = ROUTE_TM
    grid_spec = pltpu.PrefetchScalarGridSpec(
        num_scalar_prefetch=2,
        grid=(t // tm,),
        in_specs=[pl.BlockSpec((tm, LANES), lambda i, *_: (i, 0)),
                  pl.BlockSpec((tm, LANES), lambda i, *_: (i, 0)),
                  pl.BlockSpec((tm, d), lambda i, *_: (i, 0)),
                  pl.BlockSpec((1, 1, d), lambda i, *_: (i // tiles_per_batch, 0, 0)),
                  pl.BlockSpec(memory_space=pl.ANY)],
        out_specs=pl.BlockSpec((tm, d), lambda i, *_: (i, 0)),
        scratch_shapes=[pltpu.VMEM((2, TILE_SLOTS, d), ys.dtype), pltpu.SemaphoreType.DMA((2,))],
    )
    return pl.pallas_call(
        _combine_kernel,
        out_shape=jax.ShapeDtypeStruct((t, d), jnp.float32),
        grid_spec=grid_spec,
        compiler_params=pltpu.CompilerParams(vmem_limit_bytes=VMEM_LIMIT),
        name="combine",
    )(used, dest, route, gates, x1, g2, ys)


def _score_bound(q_gain, k_gain):
    rounding = 1.02
    return HEAD_DIM * jnp.max(jnp.abs(q_gain)) * jnp.max(jnp.abs(k_gain)) * Q_SCALE * rounding

def _rope_tables(s):
    pos = jnp.arange(s, dtype=jnp.int32)
    inv = 1.0 / (ROPE_BASE ** (jnp.arange(ROPE_FREQS, dtype=jnp.float32) / ROPE_FREQS))
    ang_r = (pos // GRID_W).astype(jnp.float32)[:, None] * inv
    ang_c = (pos % GRID_W).astype(jnp.float32)[:, None] * inv
    ang = jnp.concatenate([ang_r, ang_r, ang_c, ang_c], axis=-1)
    sign = jnp.asarray(np.tile(np.repeat([-1.0, 1.0], ROPE_FREQS), 2), jnp.float32)
    cos_t = jnp.tile(jnp.cos(ang), (1, LANES // HEAD_DIM))
    sin_t = jnp.tile(jnp.sin(ang) * sign, (1, LANES // HEAD_DIM))
    return cos_t, sin_t


def kernel(x, c, ctx, c_ctx, w_ada, b_ada, g_attn, w_in, q_norm_diff, k_norm_diff, lam_q1, lam_k1, lam_q2,
           lam_k2, subln_diff, q_norm_na, k_norm_na, rpb_na, out_norm_na, w_out, g_ffn, w_router, b_router,
           w_gate_up, b_gate_up, w_down, b_down):
    depth = w_ada.shape[0]
    assert depth == 1, "single-layer kernel"
    b, s, d = x.shape
    n_ctx = ctx.shape[1]
    rows_n = s // GRID_W
    assert rows_n >= NA_ROWS and s % PROJ_TM == 0 and rows_n % NA_ROWS_PER_STEP == 0
    lam_init = 0.8 - 0.6 * math.exp(-0.3 * 0)

    rows = -(-(b + 1) // 8) * 8
    cv = jnp.zeros((rows, d), jnp.float32).at[:b].set(c).at[b].set(c_ctx)
    mod = _adaln(cv, w_ada[0], b_ada[0]).reshape(rows, 6, d)
    lat = [mod[:b, i][:, None, :] for i in range(6)]
    cxm = [jnp.broadcast_to(mod[b, i][None, None, :], (b, 1, d)) for i in range(6)]
    sh1, sc1, g1, sh2, sc2, g2 = lat

    w_in_b = w_in[0].astype(jnp.bfloat16)
    gidx = np.arange(256) // HEAD_DIM
    gsum = jnp.asarray(gidx[:, None] == gidx[None, :], jnp.bfloat16)
    tile4 = lambda v, reps: jnp.tile(v.reshape(1, -1), (1, reps))
    norms = (tile4(q_norm_diff[0], 8), tile4(k_norm_diff[0], 8), tile4(q_norm_na[0], 8), tile4(k_norm_na[0], 8))
    cos_t, sin_t = _rope_tables(s)
    g_attn2 = g_attn[0].reshape(1, d)

    qd, kd, vdt, qn, kn, vn = _proj(x, g_attn2, sh1, sc1, w_in_b, gsum, cos_t, sin_t, norms,
                                    groups=("qd", "kd", "vd", "qn", "kn", "vn"), rope=True, tm=PROJ_TM)
    kd_c, vdt_c, kn_c, vn_c = _proj(ctx, g_attn2, cxm[0], cxm[1], w_in_b, gsum, cos_t[:n_ctx], sin_t[:n_ctx],
                                    norms, groups=("kd", "vd", "kn", "vn"), rope=False, tm=n_ctx)

    lams = tuple(v[0].reshape(1, HEAD_DIM) for v in (lam_q1, lam_k1, lam_q2, lam_k2))
    diff_args = (qd, kd, kd_c, vdt, vdt_c, lams, subln_diff[0].reshape(1, LANES))
    o_diff = lax.cond(
        _score_bound(q_norm_diff[0], k_norm_diff[0]) <= EXP2_SAFE_SCORE,
        lambda a: _diff_attn(*a, lam_init=lam_init, stabilise=False),
        lambda a: _diff_attn(*a, lam_init=lam_init, stabilise=True), diff_args)
    na_args = (qn, kn, vn, kn_c, vn_c, _na_bias_table(rpb_na[0]), out_norm_na[0].reshape(1, -1))
    na_bound = _score_bound(q_norm_na[0], k_norm_na[0]) + jnp.max(jnp.abs(rpb_na[0])) * LOG2E
    o_na = lax.cond(
        na_bound <= EXP2_SAFE_SCORE,
        lambda a: _na_attn(*a, stabilise=False),
        lambda a: _na_attn(*a, stabilise=True), na_args)

    x1, h2, route, gates, counts = _out_route(o_diff, o_na, x, w_out[0].astype(jnp.bfloat16), g1,
                                              g_ffn[0].reshape(1, d), sh2, sc2, w_router[0], b_router[0])

    t = b * s
    n_exp = w_router.shape[2]
    assert n_exp == N_EXPERTS
    tile_cnt = counts[:, 0, :n_exp].astype(jnp.int32)
    tile_run = (tile_cnt + RUN_ALIGN - 1) // RUN_ALIGN * RUN_ALIGN
    totals = jnp.sum(tile_run, axis=0)
    padded = (totals + EXPERT_TM - 1) // EXPERT_TM * EXPERT_TM
    pad_ends = jnp.cumsum(padded)
    pad_starts = pad_ends - padded
    tile_loc = jnp.cumsum(tile_run, axis=1) - tile_run
    tile_glob = pad_starts[None, :] + jnp.cumsum(tile_run, axis=0) - tile_run
    max_rows = t * TOP_K + tile_cnt.size * (RUN_ALIGN - 1)
    n_tiles = -(-max_rows // EXPERT_TM) + n_exp
    tile_row0 = jnp.arange(n_tiles, dtype=jnp.int32) * EXPERT_TM
    block_expert = jnp.minimum(jnp.sum(tile_row0[:, None] >= pad_ends[None, :], axis=1), n_exp - 1).astype(jnp.int32)
    n_valid = (pad_ends[-1:] // EXPERT_TM).astype(jnp.int32)
    run_end = tile_loc + tile_run
    chunk_row = (jnp.arange(TILE_SLOTS // RUN_ALIGN, dtype=jnp.int32) * RUN_ALIGN)[None, :, None]
    in_run = (chunk_row >= tile_loc[:, None, :]) & (chunk_row < run_end[:, None, :])
    chunk_dest = jnp.sum(jnp.where(in_run, (tile_glob - tile_loc)[:, None, :], 0), axis=-1) + chunk_row[:, :, 0]
    chunks_used = run_end[:, -1] // RUN_ALIGN
    moves = (chunks_used.astype(jnp.int32), chunk_dest.reshape(-1).astype(jnp.int32))
    pad_row = (pad_starts + totals).astype(jnp.int32)
    pad_chunks = ((padded - totals) // RUN_ALIGN).astype(jnp.int32)

    xs = _dispatch(*moves, pad_row, pad_chunks, n_valid, route, h2, n_tiles * EXPERT_TM)
    ys = _experts(block_expert, n_valid, xs, w_gate_up[0], b_gate_up[0], w_down[0], b_down[0])
    out = _combine(*moves, route, gates, x1, g2, ys, s // ROUTE_TM)
    return out.reshape(b, s, d)
```

```python
import functools
import math

import jax
import jax.numpy as jnp
import numpy as np
from jax import lax
from jax.experimental import pallas as pl
from jax.experimental.pallas import tpu as pltpu

GRID_W = 64
HEAD_DIM = 64
NA_ROWS = 8
NA_COLS = 16
ROPE_BASE = 10000.0
ROPE_FREQS = HEAD_DIM // 4
N_EXPERTS = 32
TOP_K = 4
SWIGLU_ALPHA = 1.702
SWIGLU_LIMIT = 7.0
NORM_EPS = 1e-6

LANES = 128
NEG = -1e30
LOG2E = math.log2(math.e)
Q_SCALE = HEAD_DIM ** -0.5 * LOG2E
EXP2_SAFE_SCORE = 50.0

PROJ_TM = 512
DIFF_TQ = 1024
DIFF_STATIC_KLOOP = True
DIFF_SCORE_BUFS = 4
NA_ROWS_PER_STEP = 64
NA_ROW_UNROLL = 64
ROUTE_TM = 512
EXPERT_TM = 512
RUN_ALIGN = 16
TILE_SLOTS = ROUTE_TM * TOP_K + N_EXPERTS * RUN_ALIGN
CHUNK_UNROLL = 8
VMEM_LIMIT = 48 * 1024 * 1024
EXPERTS_VMEM_LIMIT = 56 * 1024 * 1024

_NT = (((1,), (1,)), ((), ()))


def _f32(x):
    return x.astype(jnp.float32)


def _adaln_kernel(cv_ref, w_ref, b_ref, o_ref):
    cv = cv_ref[...]
    act = cv * (1.0 / (1.0 + jnp.exp(-cv)))
    o_ref[...] = jnp.dot(act, w_ref[...], precision=lax.Precision.HIGHEST,
                         preferred_element_type=jnp.float32) + b_ref[...]


def _adaln(cv, w_ada, b_ada):
    rows, d = cv.shape
    n = w_ada.shape[1]
    tn = 1024
    return pl.pallas_call(
        _adaln_kernel,
        out_shape=jax.ShapeDtypeStruct((rows, n), jnp.float32),
        grid=(n // tn,),
        in_specs=[pl.BlockSpec((rows, d), lambda j: (0, 0)),
                  pl.BlockSpec((d, tn), lambda j: (0, j)),
                  pl.BlockSpec((1, tn), lambda j: (0, j))],
        out_specs=pl.BlockSpec((rows, tn), lambda j: (0, j)),
        compiler_params=pltpu.CompilerParams(vmem_limit_bytes=VMEM_LIMIT),
        name="adaln",
    )(cv, w_ada, b_ada.reshape(1, n))


def _proj_kernel(*refs, groups, rope, d_model):
    (x_ref, g_ref, sh_ref, sc_ref, w_ref, gsum_ref, cos_ref, sin_ref,
     qnd_ref, knd_ref, qnn_ref, knn_ref) = refs[:12]
    outs = dict(zip(groups, refs[12:]))
    width = 4 * LANES

    x = x_ref[0]
    y = x * lax.rsqrt(jnp.mean(x * x, axis=-1, keepdims=True) + NORM_EPS) * g_ref[...]
    h = (y * (1.0 + sc_ref[0]) + sh_ref[0]).astype(jnp.bfloat16)

    group_col = {"qd": 0, "kd": 1, "vd": 2, "qn": 3, "kn": 4, "vn": 5}
    gains = {"qd": qnd_ref[...] * Q_SCALE, "kd": knd_ref[...],
             "qn": qnn_ref[...] * Q_SCALE, "kn": knn_ref[...]}
    lane = lax.broadcasted_iota(jnp.int32, (x.shape[0], LANES), 1)
    first_half = (lane % (2 * ROPE_FREQS)) < ROPE_FREQS

    for name in groups:
        c0 = group_col[name] * width
        p = jnp.dot(h, w_ref[:, c0:c0 + width], preferred_element_type=jnp.float32)
        if name in gains:
            blocks = []
            for j in range(width // 256):
                blk = p[:, j * 256:(j + 1) * 256]
                ss = jnp.dot((blk * blk).astype(jnp.bfloat16), gsum_ref[...],
                             preferred_element_type=jnp.float32)
                blocks.append(blk * lax.rsqrt(ss * (1.0 / HEAD_DIM) + NORM_EPS))
            p = jnp.concatenate(blocks, axis=1) * gains[name]
        if rope and name in ("qd", "kd"):
            blocks = []
            for j in range(width // LANES):
                blk = p[:, j * LANES:(j + 1) * LANES]
                partner = jnp.where(first_half,
                                    pltpu.roll(blk, LANES - ROPE_FREQS, axis=1),
                                    pltpu.roll(blk, ROPE_FREQS, axis=1))
                blocks.append(blk * cos_ref[...] + partner * sin_ref[...])
            p = jnp.concatenate(blocks, axis=1)
        if name == "vd":
            pt = p.T.reshape(width // LANES, LANES, p.shape[0])
            outs[name][0, :, 0] = pt.astype(jnp.bfloat16)
        else:
            outs[name][0] = p.astype(jnp.bfloat16)


def _proj(x, g_attn, shift, scale, w_in, gsum, cos_t, sin_t, norms, *, groups, rope, tm):
    b, s, d = x.shape
    nt = s // tm
    width = 4 * LANES
    out_shape, out_specs = [], []
    for name in groups:
        if name == "vd":
            out_shape.append(jax.ShapeDtypeStruct((b, 4, nt, LANES, tm), jnp.bfloat16))
            out_specs.append(pl.BlockSpec((1, 4, 1, LANES, tm), lambda bi, i: (bi, 0, i, 0, 0)))
        else:
            out_shape.append(jax.ShapeDtypeStruct((b, s, width), jnp.bfloat16))
            out_specs.append(pl.BlockSpec((1, tm, width), lambda bi, i: (bi, i, 0)))
    row = lambda bi, i: (0, 0)
    per_b = lambda bi, i: (bi, 0, 0)
    kern = functools.partial(_proj_kernel, groups=groups, rope=rope, d_model=d)
    return pl.pallas_call(
        kern,
        out_shape=out_shape,
        grid=(b, nt),
        in_specs=[pl.BlockSpec((1, tm, d), lambda bi, i: (bi, i, 0)),
                  pl.BlockSpec((1, d), row),
                  pl.BlockSpec((1, 1, d), per_b),
                  pl.BlockSpec((1, 1, d), per_b),
                  pl.BlockSpec(w_in.shape, row),
                  pl.BlockSpec(gsum.shape, row),
                  pl.BlockSpec((tm, LANES), lambda bi, i: (i, 0)),
                  pl.BlockSpec((tm, LANES), lambda bi, i: (i, 0)),
                  pl.BlockSpec((1, width), row), pl.BlockSpec((1, width), row),
                  pl.BlockSpec((1, width), row), pl.BlockSpec((1, width), row)],
        out_specs=out_specs,
        compiler_params=pltpu.CompilerParams(vmem_limit_bytes=VMEM_LIMIT),
        name="proj_rope" if rope else "proj_ctx",
    )(x, g_attn, shift, scale, w_in, gsum, cos_t, sin_t, *norms)


def _diff_attn_kernel(q_ref, k_ref, kc_ref, vt_ref, vtc_ref, lq1_ref, lk1_ref, lq2_ref, lk2_ref,
                      subln_ref, o_ref, m_ref, l_ref, acc_ref, *s_refs, lam_init, n_kblocks, stabilise):
    q = q_ref[0]
    lane = lax.broadcasted_iota(jnp.int32, q.shape, 1)
    zero = jnp.zeros_like(q)
    qz = (jnp.where(lane < HEAD_DIM, q, zero), jnp.where(lane >= HEAD_DIM, q, zero))

    l_ref[...] = jnp.zeros(l_ref.shape, jnp.float32)
    acc_ref[...] = jnp.zeros(acc_ref.shape, jnp.float32)

    def sum8(p):
        return jnp.sum(p.reshape(p.shape[0] // 8, 8, p.shape[1]), axis=0)

    def scores(k, mp):
        return lax.dot_general(k, qz[mp], _NT, preferred_element_type=jnp.float32)

    if stabilise:
        m_ref[...] = jnp.full(m_ref.shape, NEG, jnp.float32)

        def block(k, vt):
            for mp in range(2):
                s = scores(k, mp)
                m_old = m_ref[mp]
                m_new = jnp.maximum(m_old, jnp.max(s, axis=0, keepdims=True))
                alpha = jnp.exp2(m_old - m_new)
                p = jnp.exp2(s - m_new)
                l_ref[mp] = alpha * l_ref[mp] + sum8(p)
                acc_ref[mp] = alpha * acc_ref[mp] + jnp.dot(vt, p.astype(jnp.bfloat16),
                                                            preferred_element_type=jnp.float32)
                m_ref[mp] = m_new

        def body(i, carry):
            block(k_ref[0, i], vt_ref[0, 0, i])
            return carry

        lax.fori_loop(0, n_kblocks, body, 0)
        block(kc_ref[0], vtc_ref[0, 0, 0])
    else:
        n_bufs = len(s_refs)
        ahead = n_bufs // 2

        def produce_into(k, s_ref):
            for mp in range(2):
                s_ref[mp, :k.shape[0]] = scores(k, mp)

        def consume(s_ref, vt):
            for mp in range(2):
                p = jnp.exp2(s_ref[mp, :vt.shape[1]])
                l_ref[mp] += sum8(p)
                acc_ref[mp] += jnp.dot(vt, p.astype(jnp.bfloat16), preferred_element_type=jnp.float32)

        for i in range(ahead):
            produce_into(k_ref[0, i], s_refs[i])

        def body(j, carry):
            base = n_bufs * j
            for u in range(n_bufs):
                produce_into(k_ref[0, base + u + ahead], s_refs[(u + ahead) % n_bufs])
                consume(s_refs[u], vt_ref[0, 0, base + u])
            return carry

        n_main = 0 if DIFF_STATIC_KLOOP else (n_kblocks - ahead) // n_bufs
        lax.fori_loop(0, n_main, body, 0)
        for i in range(n_main * n_bufs, n_kblocks + 1):
            nxt = i + ahead
            if nxt <= n_kblocks:
                produce_into(kc_ref[0] if nxt == n_kblocks else k_ref[0, nxt], s_refs[nxt % n_bufs])
            consume(s_refs[i % n_bufs], vtc_ref[0, 0, 0] if i == n_kblocks else vt_ref[0, 0, i])

    lam = (jnp.exp(jnp.sum(lq1_ref[...] * lk1_ref[...], keepdims=True))
           - jnp.exp(jnp.sum(lq2_ref[...] * lk2_ref[...], keepdims=True)) + lam_init)
    l1 = jnp.sum(l_ref[0], axis=0, keepdims=True)
    l2 = jnp.sum(l_ref[1], axis=0, keepdims=True)
    o = acc_ref[0] / l1 - lam * (acc_ref[1] / l2)
    ot = o.T
    ot = ot * lax.rsqrt(jnp.mean(ot * ot, axis=-1, keepdims=True) + NORM_EPS)
    o_ref[0] = (ot * subln_ref[...] * (1.0 - lam_init)).astype(o_ref.dtype)


def _diff_attn(qd, kd, kd_c, vdt, vdt_c, lams, subln, *, lam_init, stabilise):
    b, s, width = qd.shape
    heads = width // LANES
    tk = vdt.shape[-1]
    nkb = s // tk
    c = kd_c.shape[1]
    kd4 = kd.reshape(b, nkb, tk, width)
    tq = DIFF_TQ
    assert nkb >= DIFF_SCORE_BUFS // 2 and c <= tk
    kern = functools.partial(_diff_attn_kernel, lam_init=lam_init, n_kblocks=nkb, stabilise=stabilise)
    vec = pl.BlockSpec((1, HEAD_DIM), lambda bi, h, i: (0, 0))
    return pl.pallas_call(
        kern,
        out_shape=jax.ShapeDtypeStruct((b, s, width), jnp.bfloat16),
        grid=(b, heads, s // tq),
        in_specs=[pl.BlockSpec((1, tq, LANES), lambda bi, h, i: (bi, i, h)),
                  pl.BlockSpec((1, nkb, tk, LANES), lambda bi, h, i: (bi, 0, 0, h)),
                  pl.BlockSpec((1, c, LANES), lambda bi, h, i: (bi, 0, h)),
                  pl.BlockSpec((1, 1, nkb, LANES, tk), lambda bi, h, i: (bi, h, 0, 0, 0)),
                  pl.BlockSpec((1, 1, 1, LANES, c), lambda bi, h, i: (bi, h, 0, 0, 0)),
                  vec, vec, vec, vec,
                  pl.BlockSpec((1, LANES), lambda bi, h, i: (0, 0))],
        out_specs=pl.BlockSpec((1, tq, LANES), lambda bi, h, i: (bi, i, h)),
        scratch_shapes=[pltpu.VMEM((2, 1, tq), jnp.float32),
                        pltpu.VMEM((2, 8, tq), jnp.float32),
                        pltpu.VMEM((2, LANES, tq), jnp.float32)]
                       + [pltpu.VMEM((2, tk, tq), jnp.float32)] * DIFF_SCORE_BUFS,
        compiler_params=pltpu.CompilerParams(vmem_limit_bytes=VMEM_LIMIT),
        name="diff_attn",
    )(qd, kd4, kd_c, vdt, vdt_c, *lams, subln)


def _na_bias_table(rpb):
    heads = rpb.shape[0]
    r = np.arange(2 * NA_ROWS - 1)[:, None, None]
    row_sel = (r == np.arange(NA_ROWS)[None, None, :] - np.arange(NA_ROWS)[None, :, None] + (NA_ROWS - 1))
    wq = np.arange(GRID_W)[:, None]
    wk = np.arange(GRID_W)[None, :]
    start = np.clip(wq - NA_COLS // 2, 0, GRID_W - NA_COLS)
    inside = (wk >= start) & (wk < start + NA_COLS)
    col_sel = (np.arange(2 * NA_COLS - 1)[:, None, None] == (wk - wq + (NA_COLS - 1))[None]) & inside[None]
    t = jnp.einsum("hrc,rdi,cqk->dhqik", rpb, jnp.asarray(row_sel, jnp.float32), jnp.asarray(col_sel, jnp.float32),
                   precision=lax.Precision.HIGHEST)
    t = jnp.where(inside[None, None, :, None, :], t * LOG2E, NEG)
    return t.reshape(NA_ROWS, heads // 2, 2 * GRID_W, NA_ROWS * GRID_W).astype(jnp.float32)


def _na_attn_kernel(q_ref, k_ref, v_ref, kc_ref, vc_ref, bias_ref, gain_ref, o_ref, octx_ref, lctx_ref, *,
                    rows_n, rows_per_step, stabilise):
    step = pl.program_id(2)
    win = NA_ROWS * GRID_W
    low = lax.broadcasted_iota(jnp.int32, (GRID_W, LANES), 1) < HEAD_DIM
    kc = kc_ref[0]
    vc = vc_ref[0]
    gain = gain_ref[...]

    def row_sum(p):
        return jnp.sum(p, axis=-1, keepdims=True)

    def head_lanes(q, hh):
        keep = (lax.broadcasted_iota(jnp.int32, q.shape, 1) < HEAD_DIM) == (hh == 0)
        return jnp.where(keep, q, jnp.zeros_like(q))

    if not stabilise:
        q_all = q_ref[0]
        for hh in range(2):
            p = jnp.exp2(lax.dot_general(head_lanes(q_all, hh), kc, _NT, preferred_element_type=jnp.float32))
            lctx_ref[hh] = jnp.broadcast_to(row_sum(p), lctx_ref.shape[1:])
            octx_ref[hh] = jnp.dot(p.astype(jnp.bfloat16), vc, preferred_element_type=jnp.float32)

    def row(j):
        r = step * rows_per_step + j
        r0 = jnp.clip(r - NA_ROWS // 2, 0, rows_n - NA_ROWS)
        d = r - r0
        start = pl.multiple_of(r0 * GRID_W, GRID_W)
        rows = pl.ds(pl.multiple_of(j * GRID_W, GRID_W), GRID_W)
        qrow = q_ref[0, rows, :]
        kw = k_ref[0, pl.ds(start, win), :]
        vw = v_ref[0, pl.ds(start, win), :]
        q2 = jnp.concatenate([head_lanes(qrow, 0), head_lanes(qrow, 1)], axis=0)
        s = lax.dot_general(q2, kw, _NT, preferred_element_type=jnp.float32) + bias_ref[d, 0]
        if stabilise:
            s_ctx = lax.dot_general(q2, kc, _NT, preferred_element_type=jnp.float32)
            m = jnp.maximum(jnp.max(s, axis=-1, keepdims=True), jnp.max(s_ctx, axis=-1, keepdims=True))
            p = jnp.exp2(s - m)
            p_ctx = jnp.exp2(s_ctx - m)
            l = row_sum(p) + row_sum(p_ctx)
            o2 = (jnp.dot(p.astype(jnp.bfloat16), vw, preferred_element_type=jnp.float32)
                  + jnp.dot(p_ctx.astype(jnp.bfloat16), vc, preferred_element_type=jnp.float32)) / l
            o = jnp.where(low, o2[:GRID_W], o2[GRID_W:])
        else:
            p = jnp.exp2(s)
            l = row_sum(p)
            o2 = jnp.dot(p.astype(jnp.bfloat16), vw, preferred_element_type=jnp.float32)
            o = jnp.where(low,
                          (o2[:GRID_W] + octx_ref[0, rows, :]) / (l[:GRID_W] + lctx_ref[0, rows, :]),
                          (o2[GRID_W:] + octx_ref[1, rows, :]) / (l[GRID_W:] + lctx_ref[1, rows, :]))
        sq = o * o
        ms0 = jnp.sum(jnp.where(low, sq, 0.0), axis=-1, keepdims=True) * (1.0 / HEAD_DIM)
        ms1 = jnp.sum(jnp.where(low, 0.0, sq), axis=-1, keepdims=True) * (1.0 / HEAD_DIM)
        inv = jnp.where(low, lax.rsqrt(ms0 + NORM_EPS), lax.rsqrt(ms1 + NORM_EPS))
        o_ref[0, rows, :] = (o * inv * gain).astype(o_ref.dtype)

    def row_group(jj, carry):
        for u in range(NA_ROW_UNROLL):
            row(NA_ROW_UNROLL * jj + u)
        return carry

    lax.fori_loop(0, rows_per_step // NA_ROW_UNROLL, row_group, 0)


def _na_attn(qn, kn, vn, kn_c, vn_c, bias, out_gain, *, stabilise):
    b, s, width = qn.shape
    pairs = width // LANES
    rows_n = s // GRID_W
    rps = NA_ROWS_PER_STEP
    c = kn_c.shape[1]
    tq = rps * GRID_W
    kern = functools.partial(_na_attn_kernel, rows_n=rows_n, rows_per_step=rps, stabilise=stabilise)
    whole = lambda bi, h, i: (bi, 0, h)
    return pl.pallas_call(
        kern,
        out_shape=jax.ShapeDtypeStruct((b, s, width), jnp.bfloat16),
        grid=(b, pairs, rows_n // rps),
        in_specs=[pl.BlockSpec((1, tq, LANES), lambda bi, h, i: (bi, i, h)),
                  pl.BlockSpec((1, s, LANES), whole),
                  pl.BlockSpec((1, s, LANES), whole),
                  pl.BlockSpec((1, c, LANES), whole),
                  pl.BlockSpec((1, c, LANES), whole),
                  pl.BlockSpec((NA_ROWS, 1, 2 * GRID_W, NA_ROWS * GRID_W), lambda bi, h, i: (0, h, 0, 0)),
                  pl.BlockSpec((1, LANES), lambda bi, h, i: (0, h))],
        out_specs=pl.BlockSpec((1, tq, LANES), lambda bi, h, i: (bi, i, h)),
        scratch_shapes=[pltpu.VMEM((2, tq, LANES), jnp.float32), pltpu.VMEM((2, tq, LANES), jnp.float32)],
        compiler_params=pltpu.CompilerParams(vmem_limit_bytes=VMEM_LIMIT),
        name="na_attn",
    )(qn, kn, vn, kn_c, vn_c, bias, out_gain)


def _out_route_kernel(od_ref, on_ref, x_ref, w_ref, g1_ref, gf_ref, sh_ref, sc_ref, wr_ref, br_ref,
                      x1_ref, h2_ref, route_ref, gate_ref, cnt_ref):
    half = od_ref.shape[2]
    attn = (jnp.dot(od_ref[0], w_ref[:half], preferred_element_type=jnp.float32)
            + jnp.dot(on_ref[0], w_ref[half:], preferred_element_type=jnp.float32))
    x1 = x_ref[0] + g1_ref[0] * attn
    x1_ref[...] = x1
    y = x1 * lax.rsqrt(jnp.mean(x1 * x1, axis=-1, keepdims=True) + NORM_EPS) * gf_ref[...]
    h2 = y * (1.0 + sc_ref[0]) + sh_ref[0]
    h2_hi = h2.astype(jnp.bfloat16)
    h2_ref[...] = h2_hi

    h2_lo = (h2 - h2_hi.astype(jnp.float32)).astype(jnp.bfloat16)
    wr = wr_ref[...]
    wr_hi = wr.astype(jnp.bfloat16)
    wr_lo = (wr - wr_hi.astype(jnp.float32)).astype(jnp.bfloat16)
    logits = (jnp.dot(h2_hi, wr_hi, preferred_element_type=jnp.float32)
              + jnp.dot(h2_lo, wr_hi, preferred_element_type=jnp.float32)
              + jnp.dot(h2_hi, wr_lo, preferred_element_type=jnp.float32)) + br_ref[...]
    tm = logits.shape[0]
    lane = lax.broadcasted_iota(jnp.int32, logits.shape, 1)
    lane_f = lane.astype(jnp.float32)
    vals, idxs = [], []
    cur = logits
    for _ in range(TOP_K):
        mx = jnp.max(cur, axis=-1, keepdims=True)
        ik = jnp.min(jnp.where(cur == mx, lane_f, float(LANES)), axis=-1, keepdims=True)
        vals.append(mx)
        idxs.append(ik)
        cur = jnp.where(lane_f == ik, -3e38, cur)
    exps = [jnp.exp(v - vals[0]) for v in vals]
    denom = exps[0] + exps[1] + exps[2] + exps[3]

    onehot = jnp.zeros(logits.shape, jnp.float32)
    for ik in idxs:
        onehot = onehot + jnp.where(lane_f == ik, 1.0, 0.0)
    ti = lax.broadcasted_iota(jnp.int32, (tm, tm), 0)
    tj = lax.broadcasted_iota(jnp.int32, (tm, tm), 1)
    lower = jnp.where(tj < ti, 1.0, 0.0).astype(jnp.bfloat16)
    prefix = jnp.dot(lower, onehot.astype(jnp.bfloat16), preferred_element_type=jnp.float32)
    counts = jnp.sum(onehot, axis=0, keepdims=True)
    run = jnp.ceil(counts * (1.0 / RUN_ALIGN)) * RUN_ALIGN
    ei = lax.broadcasted_iota(jnp.int32, (LANES, LANES), 0)
    ej = lax.broadcasted_iota(jnp.int32, (LANES, LANES), 1)
    before = jnp.where(ei < ej, 1.0, 0.0)
    starts = jnp.dot(jnp.broadcast_to(run, (8, LANES)), before, precision=lax.Precision.HIGHEST,
                     preferred_element_type=jnp.float32)[0:1]
    slot = prefix + starts
    route = jnp.zeros(logits.shape, jnp.float32)
    gates = jnp.zeros(logits.shape, jnp.float32)
    for k in range(TOP_K):
        pos = jnp.sum(jnp.where(lane_f == idxs[k], slot, 0.0), axis=-1, keepdims=True)
        route = jnp.where(lane == k, idxs[k], route)
        route = jnp.where(lane == TOP_K + k, pos, route)
        gates = jnp.where(lane == k, exps[k] / denom, gates)
    route_ref[...] = route.astype(jnp.int32)
    gate_ref[...] = gates
    cnt_ref[0] = counts


def _out_route(o_diff, o_na, x, w_out, g1, g_ffn, sh2, sc2, w_router, b_router):
    b, s, d = x.shape
    tm = ROUTE_TM
    half = o_diff.shape[2]
    e = w_router.shape[1]
    wr = jnp.zeros((d, LANES), jnp.float32).at[:, :e].set(w_router)
    br = jnp.full((1, LANES), NEG, jnp.float32).at[0, :e].set(b_router)
    tile = lambda bi, i: (bi, i, 0)
    flat = lambda bi, i: (bi * (s // tm) + i, 0)
    row = lambda bi, i: (0, 0)
    per_b = lambda bi, i: (bi, 0, 0)
    return pl.pallas_call(
        _out_route_kernel,
        out_shape=[jax.ShapeDtypeStruct((b * s, d), jnp.float32),
                   jax.ShapeDtypeStruct((b * s, d), jnp.bfloat16),
                   jax.ShapeDtypeStruct((b * s, LANES), jnp.int32),
                   jax.ShapeDtypeStruct((b * s, LANES), jnp.float32),
                   jax.ShapeDtypeStruct((b * (s // tm), 1, LANES), jnp.float32)],
        grid=(b, s // tm),
        in_specs=[pl.BlockSpec((1, tm, half), tile),
                  pl.BlockSpec((1, tm, half), tile),
                  pl.BlockSpec((1, tm, d), tile),
                  pl.BlockSpec(w_out.shape, row),
                  pl.BlockSpec((1, 1, d), per_b),
                  pl.BlockSpec((1, d), row),
                  pl.BlockSpec((1, 1, d), per_b),
                  pl.BlockSpec((1, 1, d), per_b),
                  pl.BlockSpec((d, LANES), row),
                  pl.BlockSpec((1, LANES), row)],
        out_specs=[pl.BlockSpec((tm, d), flat),
                   pl.BlockSpec((tm, d), flat),
                   pl.BlockSpec((tm, LANES), flat),
                   pl.BlockSpec((tm, LANES), flat),
                   pl.BlockSpec((1, 1, LANES), lambda bi, i: (bi * (s // tm) + i, 0, 0))],
        compiler_params=pltpu.CompilerParams(vmem_limit_bytes=VMEM_LIMIT),
        name="out_route",
    )(o_diff, o_na, x, w_out, g1, g_ffn, sh2, sc2, wr, br)


def _chunk_copies(tile, used_ref, dest_ref, local_ref, global_ref, sem, *, to_global, wait):
    n_chunks = local_ref.shape[0] // RUN_ALIGN

    def one(c, carry):
        lref = local_ref.at[pl.ds(pl.multiple_of(c * RUN_ALIGN, RUN_ALIGN), RUN_ALIGN), :]
        gref = global_ref.at[pl.ds(pl.multiple_of(dest_ref[tile * n_chunks + c], RUN_ALIGN), RUN_ALIGN), :]
        cp = pltpu.make_async_copy(lref, gref, sem) if to_global else pltpu.make_async_copy(gref, lref, sem)
        cp.wait() if wait else cp.start()
        return carry

    def group(g, carry):
        for u in range(CHUNK_UNROLL):
            one(g * CHUNK_UNROLL + u, carry)
        return carry

    used = used_ref[tile]
    n_groups = used // CHUNK_UNROLL
    lax.fori_loop(0, n_groups, group, 0)
    lax.fori_loop(n_groups * CHUNK_UNROLL, used, one, 0)


def _slot_matrix(route, values, n_slots):
    width = 2 * LANES
    blocks = []
    for first in range(0, n_slots, width):
        slots = lax.broadcasted_iota(jnp.int32, (route.shape[0], width), 1) + first
        block = jnp.zeros(slots.shape, jnp.float32)
        for k in range(TOP_K):
            block = jnp.where(slots == route[:, TOP_K + k:TOP_K + k + 1], values[k], block)
        blocks.append(block.astype(jnp.bfloat16))
    return jnp.concatenate(blocks, axis=1)


def _zero_unsorted_rows(pad_row_ref, pad_chunks_ref, nv_ref, zero_ref, xs_ref, sem):
    zero_ref[...] = jnp.zeros(zero_ref.shape, zero_ref.dtype)
    n_tiles = xs_ref.shape[0] // EXPERT_TM
    for wait in (False, True):
        def pad_chunk(e, j, c):
            row = pl.multiple_of(pad_row_ref[e] + j * RUN_ALIGN, RUN_ALIGN)
            cp = pltpu.make_async_copy(zero_ref.at[pl.ds(0, RUN_ALIGN), :], xs_ref.at[pl.ds(row, RUN_ALIGN), :], sem)
            cp.wait() if wait else cp.start()
            return c

        def unused_tile(i, c):
            row = pl.multiple_of(i * EXPERT_TM, EXPERT_TM)
            cp = pltpu.make_async_copy(zero_ref, xs_ref.at[pl.ds(row, EXPERT_TM), :], sem)
            cp.wait() if wait else cp.start()
            return c

        lax.fori_loop(0, N_EXPERTS, lambda e, c: lax.fori_loop(
            0, pad_chunks_ref[e], functools.partial(pad_chunk, e), c), 0)
        lax.fori_loop(nv_ref[0], n_tiles, unused_tile, 0)


def _dispatch_kernel(used_ref, dest_ref, pad_row_ref, pad_chunks_ref, nv_ref, route_ref, h_ref, xs_ref,
                     buf_ref, zero_ref, sems, zero_sem):
    tile = pl.program_id(0)
    slot = tile % 2

    @pl.when(tile == 0)
    def _():
        _zero_unsorted_rows(pad_row_ref, pad_chunks_ref, nv_ref, zero_ref, xs_ref, zero_sem)

    route = route_ref[...]
    n_slots = buf_ref.shape[1]
    perm = _slot_matrix(route, (1.0,) * TOP_K, n_slots)
    rows = lax.dot_general(perm, h_ref[...], (((0,), (0,)), ((), ())),
                           preferred_element_type=jnp.float32)
    buf_ref[slot] = rows.astype(buf_ref.dtype)

    def copies(t, s, wait):
        _chunk_copies(t, used_ref, dest_ref, buf_ref.at[s], xs_ref, sems.at[s], to_global=True, wait=wait)

    copies(tile, slot, False)

    @pl.when(tile > 0)
    def _():
        copies(tile - 1, 1 - slot, True)

    @pl.when(tile == pl.num_programs(0) - 1)
    def _():
        copies(tile, slot, True)


def _dispatch(used, dest, pad_row, pad_chunks, n_valid, route, h2, n_rows):
    t, d = h2.shape
    tm = ROUTE_TM
    grid_spec = pltpu.PrefetchScalarGridSpec(
        num_scalar_prefetch=5,
        grid=(t // tm,),
        in_specs=[pl.BlockSpec((tm, LANES), lambda i, *_: (i, 0)),
                  pl.BlockSpec((tm, d), lambda i, *_: (i, 0))],
        out_specs=pl.BlockSpec(memory_space=pl.ANY),
        scratch_shapes=[pltpu.VMEM((2, TILE_SLOTS, d), h2.dtype), pltpu.VMEM((EXPERT_TM, d), h2.dtype),
                        pltpu.SemaphoreType.DMA((2,)), pltpu.SemaphoreType.DMA(())],
    )
    return pl.pallas_call(
        _dispatch_kernel,
        out_shape=jax.ShapeDtypeStruct((n_rows, d), h2.dtype),
        grid_spec=grid_spec,
        compiler_params=pltpu.CompilerParams(vmem_limit_bytes=VMEM_LIMIT),
        name="dispatch",
    )(used, dest, pad_row, pad_chunks, n_valid, route, h2)


def _experts_kernel(be_ref, nv_ref, xs_ref, wgu_ref, bgu_ref, wdn_ref, bdn_ref, ys_ref, wgu_b_ref, wdn_b_ref):
    i = pl.program_id(0)
    valid = i < nv_ref[0]
    new_expert = (i == 0) | (be_ref[i] != be_ref[jnp.maximum(i - 1, 0)])

    @pl.when(valid & new_expert)
    def _():
        wgu_b_ref[...] = wgu_ref[0].astype(jnp.bfloat16)
        wdn_b_ref[...] = wdn_ref[0].astype(jnp.bfloat16)

    @pl.when(valid)
    def _():
        gu = jnp.dot(xs_ref[...], wgu_b_ref[...], preferred_element_type=jnp.float32) + bgu_ref[0]
        f = gu.shape[1] // 2
        glu = jnp.minimum(gu[:, :f], SWIGLU_LIMIT)
        lin = jnp.clip(gu[:, f:], -SWIGLU_LIMIT, SWIGLU_LIMIT)
        act = glu * (1.0 / (1.0 + jnp.exp(-SWIGLU_ALPHA * glu))) * (lin + 1.0)
        y = jnp.dot(act.astype(jnp.bfloat16), wdn_b_ref[...], preferred_element_type=jnp.float32) + bdn_ref[0]
        ys_ref[...] = y.astype(ys_ref.dtype)

    @pl.when(jnp.logical_not(valid))
    def _():
        ys_ref[...] = jnp.zeros(ys_ref.shape, ys_ref.dtype)


def _experts(block_expert, n_valid, xs, w_gu, b_gu, w_dn, b_dn):
    n_rows, dw = xs.shape
    tm = EXPERT_TM
    e, d, f2 = w_gu.shape
    grid_spec = pltpu.PrefetchScalarGridSpec(
        num_scalar_prefetch=2,
        grid=(n_rows // tm,),
        in_specs=[pl.BlockSpec((tm, dw), lambda i, be, nv: (jnp.clip(i, 0, jnp.maximum(nv[0] - 1, 0)), 0)),
                  pl.BlockSpec((1, d, f2), lambda i, be, nv: (be[i], 0, 0)),
                  pl.BlockSpec((1, 1, f2), lambda i, be, nv: (be[i], 0, 0)),
                  pl.BlockSpec((1, f2 // 2, d), lambda i, be, nv: (be[i], 0, 0)),
                  pl.BlockSpec((1, 1, d), lambda i, be, nv: (be[i], 0, 0))],
        out_specs=pl.BlockSpec((tm, d), lambda i, be, nv: (i, 0)),
        scratch_shapes=[pltpu.VMEM((d, f2), jnp.bfloat16), pltpu.VMEM((f2 // 2, d), jnp.bfloat16)],
    )
    return pl.pallas_call(
        _experts_kernel,
        out_shape=jax.ShapeDtypeStruct((n_rows, d), jnp.bfloat16),
        grid_spec=grid_spec,
        compiler_params=pltpu.CompilerParams(vmem_limit_bytes=EXPERTS_VMEM_LIMIT),
        name="experts",
    )(block_expert, n_valid, xs, w_gu, b_gu.reshape(e, 1, f2), w_dn, b_dn.reshape(e, 1, d))


def _combine_kernel(used_ref, dest_ref, route_ref, gate_ref, x1_ref, g2_ref, ys_ref, o_ref, buf_ref, sems):
    tile = pl.program_id(0)
    slot = tile % 2
    n_slots = buf_ref.shape[1]
    covered = route_ref.shape[0] * TOP_K

    def copies(t, s, wait):
        _chunk_copies(t, used_ref, dest_ref, buf_ref.at[s], ys_ref, sems.at[s], to_global=False, wait=wait)

    def fetch(t, s):
        buf_ref[s, covered:] = jnp.zeros((n_slots - covered, buf_ref.shape[2]), buf_ref.dtype)
        copies(t, s, False)

    @pl.when(tile == 0)
    def _():
        fetch(0, 0)

    @pl.when(tile + 1 < pl.num_programs(0))
    def _():
        fetch(tile + 1, 1 - slot)

    route = route_ref[...]
    gates = gate_ref[...]
    copies(tile, slot, True)
    weights = _slot_matrix(route, [gates[:, k:k + 1] for k in range(TOP_K)], n_slots)
    moe = jnp.dot(weights, buf_ref[slot], preferred_element_type=jnp.float32)
    o_ref[...] = x1_ref[...] + g2_ref[0] * moe


def _combine(used, dest, route, gates, x1, g2, ys, tiles_per_batch):
    t, d = x1.shape
    tm = ROUTE_TM
    grid_spec = pltpu.PrefetchScalarGridSpec(
        num_scalar_prefetch=2,
        grid=(t // tm,),
        in_specs=[pl.BlockSpec((tm, LANES), lambda i, *_: (i, 0)),
                  pl.BlockSpec((tm, LANES), lambda i, *_: (i, 0)),
                  pl.BlockSpec((tm, d), lambda i, *_: (i, 0)),
                  pl.BlockSpec((1, 1, d), lambda i, *_: (i // tiles_per_batch, 0, 0)),
                  pl.BlockSpec(memory_space=pl.ANY)],
        out_specs=pl.BlockSpec((tm, d), lambda i, *_: (i, 0)),
        scratch_shapes=[pltpu.VMEM((2, TILE_SLOTS, d), ys.dtype), pltpu.SemaphoreType.DMA((2,))],
    )
    return pl.pallas_call(
        _combine_kernel,
        out_shape=jax.ShapeDtypeStruct((t, d), jnp.float32),
        grid_spec=grid_spec,
        compiler_params=pltpu.CompilerParams(vmem_limit_bytes=VMEM_LIMIT),
        name="combine",
    )(used, dest, route, gates, x1, g2, ys)


def _score_bound(q_gain, k_gain):
    rounding = 1.02
    return HEAD_DIM * jnp.max(jnp.abs(q_gain)) * jnp.max(jnp.abs(k_gain)) * Q_SCALE * rounding

def _rope_tables(s):
    pos = jnp.arange(s, dtype=jnp.int32)
    inv = 1.0 / (ROPE_BASE ** (jnp.arange(ROPE_FREQS, dtype=jnp.float32) / ROPE_FREQS))
    ang_r = (pos // GRID_W).astype(jnp.float32)[:, None] * inv
    ang_c = (pos % GRID_W).astype(jnp.float32)[:, None] * inv
    ang = jnp.concatenate([ang_r, ang_r, ang_c, ang_c], axis=-1)
    sign = jnp.asarray(np.tile(np.repeat([-1.0, 1.0], ROPE_FREQS), 2), jnp.float32)
    cos_t = jnp.tile(jnp.cos(ang), (1, LANES // HEAD_DIM))
    sin_t = jnp.tile(jnp.sin(ang) * sign, (1, LANES // HEAD_DIM))
    return cos_t, sin_t


def kernel(x, c, ctx, c_ctx, w_ada, b_ada, g_attn, w_in, q_norm_diff, k_norm_diff, lam_q1, lam_k1, lam_q2,
           lam_k2, subln_diff, q_norm_na, k_norm_na, rpb_na, out_norm_na, w_out, g_ffn, w_router, b_router,
           w_gate_up, b_gate_up, w_down, b_down):
    depth = w_ada.shape[0]
    assert depth == 1, "single-layer kernel"
    b, s, d = x.shape
    n_ctx = ctx.shape[1]
    rows_n = s // GRID_W
    assert rows_n >= NA_ROWS and s % PROJ_TM == 0 and rows_n % NA_ROWS_PER_STEP == 0
    lam_init = 0.8 - 0.6 * math.exp(-0.3 * 0)

    rows = -(-(b + 1) // 8) * 8
    cv = jnp.zeros((rows, d), jnp.float32).at[:b].set(c).at[b].set(c_ctx)
    mod = _adaln(cv, w_ada[0], b_ada[0]).reshape(rows, 6, d)
    lat = [mod[:b, i][:, None, :] for i in range(6)]
    cxm = [jnp.broadcast_to(mod[b, i][None, None, :], (b, 1, d)) for i in range(6)]
    sh1, sc1, g1, sh2, sc2, g2 = lat

    w_in_b = w_in[0].astype(jnp.bfloat16)
    gidx = np.arange(256) // HEAD_DIM
    gsum = jnp.asarray(gidx[:, None] == gidx[None, :], jnp.bfloat16)
    tile4 = lambda v, reps: jnp.tile(v.reshape(1, -1), (1, reps))
    norms = (tile4(q_norm_diff[0], 8), tile4(k_norm_diff[0], 8), tile4(q_norm_na[0], 8), tile4(k_norm_na[0], 8))
    cos_t, sin_t = _rope_tables(s)
    g_attn2 = g_attn[0].reshape(1, d)

    qd, kd, vdt, qn, kn, vn = _proj(x, g_attn2, sh1, sc1, w_in_b, gsum, cos_t, sin_t, norms,
                                    groups=("qd", "kd", "vd", "qn", "kn", "vn"), rope=True, tm=PROJ_TM)
    kd_c, vdt_c, kn_c, vn_c = _proj(ctx, g_attn2, cxm[0], cxm[1], w_in_b, gsum, cos_t[:n_ctx], sin_t[:n_ctx],
                                    norms, groups=("kd", "vd", "kn", "vn"), rope=False, tm=n_ctx)

    lams = tuple(v[0].reshape(1, HEAD_DIM) for v in (lam_q1, lam_k1, lam_q2, lam_k2))
    diff_args = (qd, kd, kd_c, vdt, vdt_c, lams, subln_diff[0].reshape(1, LANES))
    o_diff = lax.cond(
        _score_bound(q_norm_diff[0], k_norm_diff[0]) <= EXP2_SAFE_SCORE,
        lambda a: _diff_attn(*a, lam_init=lam_init, stabilise=False),
        lambda a: _diff_attn(*a, lam_init=lam_init, stabilise=True), diff_args)
    na_args = (qn, kn, vn, kn_c, vn_c, _na_bias_table(rpb_na[0]), out_norm_na[0].reshape(1, -1))
    na_bound = _score_bound(q_norm_na[0], k_norm_na[0]) + jnp.max(jnp.abs(rpb_na[0])) * LOG2E
    o_na = lax.cond(
        na_bound <= EXP2_SAFE_SCORE,
        lambda a: _na_attn(*a, stabilise=False),
        lambda a: _na_attn(*a, stabilise=True), na_args)

    x1, h2, route, gates, counts = _out_route(o_diff, o_na, x, w_out[0].astype(jnp.bfloat16), g1,
                                              g_ffn[0].reshape(1, d), sh2, sc2, w_router[0], b_router[0])

    t = b * s
    n_exp = w_router.shape[2]
    assert n_exp == N_EXPERTS
    tile_cnt = counts[:, 0, :n_exp].astype(jnp.int32)
    tile_run = (tile_cnt + RUN_ALIGN - 1) // RUN_ALIGN * RUN_ALIGN
    totals = jnp.sum(tile_run, axis=0)
    padded = (totals + EXPERT_TM - 1) // EXPERT_TM * EXPERT_TM
    pad_ends = jnp.cumsum(padded)
    pad_starts = pad_ends - padded
    tile_loc = jnp.cumsum(tile_run, axis=1) - tile_run
    tile_glob = pad_starts[None, :] + jnp.cumsum(tile_run, axis=0) - tile_run
    max_rows = t * TOP_K + tile_cnt.size * (RUN_ALIGN - 1)
    n_tiles = -(-max_rows // EXPERT_TM) + n_exp
    tile_row0 = jnp.arange(n_tiles, dtype=jnp.int32) * EXPERT_TM
    block_expert = jnp.minimum(jnp.sum(tile_row0[:, None] >= pad_ends[None, :], axis=1), n_exp - 1).astype(jnp.int32)
    n_valid = (pad_ends[-1:] // EXPERT_TM).astype(jnp.int32)
    run_end = tile_loc + tile_run
    chunk_row = (jnp.arange(TILE_SLOTS // RUN_ALIGN, dtype=jnp.int32) * RUN_ALIGN)[None, :, None]
    in_run = (chunk_row >= tile_loc[:, None, :]) & (chunk_row < run_end[:, None, :])
    chunk_dest = jnp.sum(jnp.where(in_run, (tile_glob - tile_loc)[:, None, :], 0), axis=-1) + chunk_row[:, :, 0]
    chunks_used = run_end[:, -1] // RUN_ALIGN
    moves = (chunks_used.astype(jnp.int32), chunk_dest.reshape(-1).astype(jnp.int32))
    pad_row = (pad_starts + totals).astype(jnp.int32)
    pad_chunks = ((padded - totals) // RUN_ALIGN).astype(jnp.int32)

    xs = _dispatch(*moves, pad_row, pad_chunks, n_valid, route, h2, n_tiles * EXPERT_TM)
    ys = _experts(block_expert, n_valid, xs, w_gate_up[0], b_gate_up[0], w_down[0], b_down[0])
    out = _combine(*moves, route, gates, x1, g2, ys, s // ROUTE_TM)
    return out.reshape(b, s, d)
```

```python
import functools
import math

import jax
import jax.numpy as jnp
import numpy as np
from jax import lax
from jax.experimental import pallas as pl
from jax.experimental.pallas import tpu as pltpu

GRID_W = 64
HEAD_DIM = 64
NA_ROWS = 8
NA_COLS = 16
ROPE_BASE = 10000.0
ROPE_FREQS = HEAD_DIM // 4
N_EXPERTS = 32
TOP_K = 4
SWIGLU_ALPHA = 1.702
SWIGLU_LIMIT = 7.0
NORM_EPS = 1e-6

LANES = 128
NEG = -1e30
LOG2E = math.log2(math.e)
Q_SCALE = HEAD_DIM ** -0.5 * LOG2E
EXP2_SAFE_SCORE = 50.0

PROJ_TM = 512
DIFF_TQ = 1024
DIFF_SCORE_BUFS = 4
NA_ROWS_PER_STEP = 64
ROUTE_TM = 512
EXPERT_TM = 512
RUN_ALIGN = 16
TILE_SLOTS = ROUTE_TM * TOP_K + N_EXPERTS * RUN_ALIGN
CHUNK_UNROLL = 8
VMEM_LIMIT = 48 * 1024 * 1024
EXPERTS_VMEM_LIMIT = 56 * 1024 * 1024

_NT = (((1,), (1,)), ((), ()))


def _adaln_kernel(cv_ref, w_ref, b_ref, o_ref):
    cv = cv_ref[...]
    act = cv * (1.0 / (1.0 + jnp.exp(-cv)))
    o_ref[...] = jnp.dot(act, w_ref[...], precision=lax.Precision.HIGHEST,
                         preferred_element_type=jnp.float32) + b_ref[...]


def _adaln(cv, w_ada, b_ada):
    rows, d = cv.shape
    n = w_ada.shape[1]
    tn = 1024
    return pl.pallas_call(
        _adaln_kernel,
        out_shape=jax.ShapeDtypeStruct((rows, n), jnp.float32),
        grid=(n // tn,),
        in_specs=[pl.BlockSpec((rows, d), lambda j: (0, 0)),
                  pl.BlockSpec((d, tn), lambda j: (0, j)),
                  pl.BlockSpec((1, tn), lambda j: (0, j))],
        out_specs=pl.BlockSpec((rows, tn), lambda j: (0, j)),
        compiler_params=pltpu.CompilerParams(vmem_limit_bytes=VMEM_LIMIT),
        name="adaln",
    )(cv, w_ada, b_ada.reshape(1, n))


def _proj_kernel(*refs, groups, rope):
    (x_ref, g_ref, sh_ref, sc_ref, w_ref, gsum_ref, cos_ref, sin_ref,
     qnd_ref, knd_ref, qnn_ref, knn_ref) = refs[:12]
    outs = dict(zip(groups, refs[12:]))
    width = 4 * LANES

    x = x_ref[0]
    y = x * lax.rsqrt(jnp.mean(x * x, axis=-1, keepdims=True) + NORM_EPS) * g_ref[...]
    h = (y * (1.0 + sc_ref[0]) + sh_ref[0]).astype(jnp.bfloat16)

    group_col = {"qd": 0, "kd": 1, "vd": 2, "qn": 3, "kn": 4, "vn": 5}
    gains = {"qd": qnd_ref[...] * Q_SCALE, "kd": knd_ref[...],
             "qn": qnn_ref[...] * Q_SCALE, "kn": knn_ref[...]}
    lane = lax.broadcasted_iota(jnp.int32, (x.shape[0], LANES), 1)
    first_half = (lane % (2 * ROPE_FREQS)) < ROPE_FREQS

    for name in groups:
        c0 = group_col[name] * width
        p = jnp.dot(h, w_ref[:, c0:c0 + width], preferred_element_type=jnp.float32)
        if name in gains:
            blocks = []
            for j in range(width // 256):
                blk = p[:, j * 256:(j + 1) * 256]
                ss = jnp.dot((blk * blk).astype(jnp.bfloat16), gsum_ref[...],
                             preferred_element_type=jnp.float32)
                blocks.append(blk * lax.rsqrt(ss * (1.0 / HEAD_DIM) + NORM_EPS))
            p = jnp.concatenate(blocks, axis=1) * gains[name]
        if rope and name in ("qd", "kd"):
            blocks = []
            for j in range(width // LANES):
                blk = p[:, j * LANES:(j + 1) * LANES]
                partner = jnp.where(first_half,
                                    pltpu.roll(blk, LANES - ROPE_FREQS, axis=1),
                                    pltpu.roll(blk, ROPE_FREQS, axis=1))
                blocks.append(blk * cos_ref[...] + partner * sin_ref[...])
            p = jnp.concatenate(blocks, axis=1)
        if name == "vd":
            pt = p.T.reshape(width // LANES, LANES, p.shape[0])
            outs[name][0, :, 0] = pt.astype(jnp.bfloat16)
        else:
            outs[name][0] = p.astype(jnp.bfloat16)


def _proj(x, g_attn, shift, scale, w_in, gsum, cos_t, sin_t, norms, *, groups, rope, tm):
    b, s, d = x.shape
    nt = s // tm
    width = 4 * LANES
    out_shape, out_specs = [], []
    for name in groups:
        if name == "vd":
            out_shape.append(jax.ShapeDtypeStruct((b, 4, nt, LANES, tm), jnp.bfloat16))
            out_specs.append(pl.BlockSpec((1, 4, 1, LANES, tm), lambda bi, i: (bi, 0, i, 0, 0)))
        else:
            out_shape.append(jax.ShapeDtypeStruct((b, s, width), jnp.bfloat16))
            out_specs.append(pl.BlockSpec((1, tm, width), lambda bi, i: (bi, i, 0)))
    row = lambda bi, i: (0, 0)
    per_b = lambda bi, i: (bi, 0, 0)
    kern = functools.partial(_proj_kernel, groups=groups, rope=rope)
    return pl.pallas_call(
        kern,
        out_shape=out_shape,
        grid=(b, nt),
        in_specs=[pl.BlockSpec((1, tm, d), lambda bi, i: (bi, i, 0)),
                  pl.BlockSpec((1, d), row),
                  pl.BlockSpec((1, 1, d), per_b),
                  pl.BlockSpec((1, 1, d), per_b),
                  pl.BlockSpec(w_in.shape, row),
                  pl.BlockSpec(gsum.shape, row),
                  pl.BlockSpec((tm, LANES), lambda bi, i: (i, 0)),
                  pl.BlockSpec((tm, LANES), lambda bi, i: (i, 0)),
                  pl.BlockSpec((1, width), row), pl.BlockSpec((1, width), row),
                  pl.BlockSpec((1, width), row), pl.BlockSpec((1, width), row)],
        out_specs=out_specs,
        compiler_params=pltpu.CompilerParams(vmem_limit_bytes=VMEM_LIMIT),
        name="proj_rope" if rope else "proj_ctx",
    )(x, g_attn, shift, scale, w_in, gsum, cos_t, sin_t, *norms)


def _diff_attn_kernel(q_ref, k_ref, kc_ref, vt_ref, vtc_ref, lq1_ref, lk1_ref, lq2_ref, lk2_ref,
                      subln_ref, o_ref, m_ref, l_ref, acc_ref, *s_refs, lam_init, n_kblocks, stabilise):
    q = q_ref[0]
    lane = lax.broadcasted_iota(jnp.int32, q.shape, 1)
    zero = jnp.zeros_like(q)
    qz = (jnp.where(lane < HEAD_DIM, q, zero), jnp.where(lane >= HEAD_DIM, q, zero))

    l_ref[...] = jnp.zeros(l_ref.shape, jnp.float32)
    acc_ref[...] = jnp.zeros(acc_ref.shape, jnp.float32)

    def sum8(p):
        return jnp.sum(p.reshape(p.shape[0] // 8, 8, p.shape[1]), axis=0)

    def scores(k, mp):
        return lax.dot_general(k, qz[mp], _NT, preferred_element_type=jnp.float32)

    if stabilise:
        m_ref[...] = jnp.full(m_ref.shape, NEG, jnp.float32)

        def block(k, vt):
            for mp in range(2):
                s = scores(k, mp)
                m_old = m_ref[mp]
                m_new = jnp.maximum(m_old, jnp.max(s, axis=0, keepdims=True))
                alpha = jnp.exp2(m_old - m_new)
                p = jnp.exp2(s - m_new)
                l_ref[mp] = alpha * l_ref[mp] + sum8(p)
                acc_ref[mp] = alpha * acc_ref[mp] + jnp.dot(vt, p.astype(jnp.bfloat16),
                                                            preferred_element_type=jnp.float32)
                m_ref[mp] = m_new

        def body(i, carry):
            block(k_ref[0, i], vt_ref[0, 0, i])
            return carry

        lax.fori_loop(0, n_kblocks, body, 0)
        block(kc_ref[0], vtc_ref[0, 0, 0])
    else:
        n_bufs = len(s_refs)
        ahead = n_bufs // 2

        def produce_into(k, s_ref):
            for mp in range(2):
                s_ref[mp, :k.shape[0]] = scores(k, mp)

        def consume(s_ref, vt):
            for mp in range(2):
                p = jnp.exp2(s_ref[mp, :vt.shape[1]])
                l_ref[mp] += sum8(p)
                acc_ref[mp] += jnp.dot(vt, p.astype(jnp.bfloat16), preferred_element_type=jnp.float32)

        for i in range(ahead):
            produce_into(k_ref[0, i], s_refs[i])

        for i in range(n_kblocks + 1):
            nxt = i + ahead
            if nxt <= n_kblocks:
                produce_into(kc_ref[0] if nxt == n_kblocks else k_ref[0, nxt], s_refs[nxt % n_bufs])
            consume(s_refs[i % n_bufs], vtc_ref[0, 0, 0] if i == n_kblocks else vt_ref[0, 0, i])

    lam = (jnp.exp(jnp.sum(lq1_ref[...] * lk1_ref[...], keepdims=True))
           - jnp.exp(jnp.sum(lq2_ref[...] * lk2_ref[...], keepdims=True)) + lam_init)
    l1 = jnp.sum(l_ref[0], axis=0, keepdims=True)
    l2 = jnp.sum(l_ref[1], axis=0, keepdims=True)
    o = acc_ref[0] / l1 - lam * (acc_ref[1] / l2)
    ot = o.T
    ot = ot * lax.rsqrt(jnp.mean(ot * ot, axis=-1, keepdims=True) + NORM_EPS)
    o_ref[0] = (ot * subln_ref[...] * (1.0 - lam_init)).astype(o_ref.dtype)


def _diff_attn(qd, kd, kd_c, vdt, vdt_c, lams, subln, *, lam_init, stabilise):
    b, s, width = qd.shape
    heads = width // LANES
    tk = vdt.shape[-1]
    nkb = s // tk
    c = kd_c.shape[1]
    kd4 = kd.reshape(b, nkb, tk, width)
    tq = DIFF_TQ
    assert nkb >= DIFF_SCORE_BUFS // 2 and c <= tk
    kern = functools.partial(_diff_attn_kernel, lam_init=lam_init, n_kblocks=nkb, stabilise=stabilise)
    vec = pl.BlockSpec((1, HEAD_DIM), lambda bi, h, i: (0, 0))
    return pl.pallas_call(
        kern,
        out_shape=jax.ShapeDtypeStruct((b, s, width), jnp.bfloat16),
        grid=(b, heads, s // tq),
        in_specs=[pl.BlockSpec((1, tq, LANES), lambda bi, h, i: (bi, i, h)),
                  pl.BlockSpec((1, nkb, tk, LANES), lambda bi, h, i: (bi, 0, 0, h)),
                  pl.BlockSpec((1, c, LANES), lambda bi, h, i: (bi, 0, h)),
                  pl.BlockSpec((1, 1, nkb, LANES, tk), lambda bi, h, i: (bi, h, 0, 0, 0)),
                  pl.BlockSpec((1, 1, 1, LANES, c), lambda bi, h, i: (bi, h, 0, 0, 0)),
                  vec, vec, vec, vec,
                  pl.BlockSpec((1, LANES), lambda bi, h, i: (0, 0))],
        out_specs=pl.BlockSpec((1, tq, LANES), lambda bi, h, i: (bi, i, h)),
        scratch_shapes=[pltpu.VMEM((2, 1, tq), jnp.float32),
                        pltpu.VMEM((2, 8, tq), jnp.float32),
                        pltpu.VMEM((2, LANES, tq), jnp.float32)]
                       + [pltpu.VMEM((2, tk, tq), jnp.float32)] * DIFF_SCORE_BUFS,
        compiler_params=pltpu.CompilerParams(vmem_limit_bytes=VMEM_LIMIT),
        name="diff_attn",
    )(qd, kd4, kd_c, vdt, vdt_c, *lams, subln)


def _na_bias_table(rpb):
    heads = rpb.shape[0]
    r = np.arange(2 * NA_ROWS - 1)[:, None, None]
    row_sel = (r == np.arange(NA_ROWS)[None, None, :] - np.arange(NA_ROWS)[None, :, None] + (NA_ROWS - 1))
    wq = np.arange(GRID_W)[:, None]
    wk = np.arange(GRID_W)[None, :]
    start = np.clip(wq - NA_COLS // 2, 0, GRID_W - NA_COLS)
    inside = (wk >= start) & (wk < start + NA_COLS)
    col_sel = (np.arange(2 * NA_COLS - 1)[:, None, None] == (wk - wq + (NA_COLS - 1))[None]) & inside[None]
    t = jnp.einsum("hrc,rdi,cqk->dhqik", rpb, jnp.asarray(row_sel, jnp.float32), jnp.asarray(col_sel, jnp.float32),
                   precision=lax.Precision.HIGHEST)
    t = jnp.where(inside[None, None, :, None, :], t * LOG2E, NEG)
    return t.reshape(NA_ROWS, heads // 2, 2 * GRID_W, NA_ROWS * GRID_W).astype(jnp.float32)


def _na_attn_kernel(q_ref, k_ref, v_ref, kc_ref, vc_ref, bias_ref, gain_ref, o_ref, octx_ref, lctx_ref, *,
                    rows_n, rows_per_step, stabilise):
    step = pl.program_id(2)
    win = NA_ROWS * GRID_W
    low = lax.broadcasted_iota(jnp.int32, (GRID_W, LANES), 1) < HEAD_DIM
    kc = kc_ref[0]
    vc = vc_ref[0]
    gain = gain_ref[...]

    def head_lanes(q, hh):
        keep = (lax.broadcasted_iota(jnp.int32, q.shape, 1) < HEAD_DIM) == (hh == 0)
        return jnp.where(keep, q, jnp.zeros_like(q))

    if not stabilise:
        q_all = q_ref[0]
        for hh in range(2):
            p = jnp.exp2(lax.dot_general(head_lanes(q_all, hh), kc, _NT, preferred_element_type=jnp.float32))
            lctx_ref[hh] = jnp.broadcast_to(jnp.sum(p, axis=-1, keepdims=True), lctx_ref.shape[1:])
            octx_ref[hh] = jnp.dot(p.astype(jnp.bfloat16), vc, preferred_element_type=jnp.float32)

    def row(j):
        r = step * rows_per_step + j
        r0 = jnp.clip(r - NA_ROWS // 2, 0, rows_n - NA_ROWS)
        d = r - r0
        start = pl.multiple_of(r0 * GRID_W, GRID_W)
        rows = pl.ds(j * GRID_W, GRID_W)
        qrow = q_ref[0, rows, :]
        kw = k_ref[0, pl.ds(start, win), :]
        vw = v_ref[0, pl.ds(start, win), :]
        q2 = jnp.concatenate([head_lanes(qrow, 0), head_lanes(qrow, 1)], axis=0)
        s = lax.dot_general(q2, kw, _NT, preferred_element_type=jnp.float32) + bias_ref[d, 0]
        if stabilise:
            s_ctx = lax.dot_general(q2, kc, _NT, preferred_element_type=jnp.float32)
            m = jnp.maximum(jnp.max(s, axis=-1, keepdims=True), jnp.max(s_ctx, axis=-1, keepdims=True))
            p = jnp.exp2(s - m)
            p_ctx = jnp.exp2(s_ctx - m)
            l = jnp.sum(p, axis=-1, keepdims=True) + jnp.sum(p_ctx, axis=-1, keepdims=True)
            o2 = (jnp.dot(p.astype(jnp.bfloat16), vw, preferred_element_type=jnp.float32)
                  + jnp.dot(p_ctx.astype(jnp.bfloat16), vc, preferred_element_type=jnp.float32)) / l
            o = jnp.where(low, o2[:GRID_W], o2[GRID_W:])
        else:
            p = jnp.exp2(s)
            l = jnp.sum(p, axis=-1, keepdims=True)
            o2 = jnp.dot(p.astype(jnp.bfloat16), vw, preferred_element_type=jnp.float32)
            o = jnp.where(low,
                          (o2[:GRID_W] + octx_ref[0, rows, :]) / (l[:GRID_W] + lctx_ref[0, rows, :]),
                          (o2[GRID_W:] + octx_ref[1, rows, :]) / (l[GRID_W:] + lctx_ref[1, rows, :]))
        sq = o * o
        ms0 = jnp.sum(jnp.where(low, sq, 0.0), axis=-1, keepdims=True) * (1.0 / HEAD_DIM)
        ms1 = jnp.sum(jnp.where(low, 0.0, sq), axis=-1, keepdims=True) * (1.0 / HEAD_DIM)
        inv = jnp.where(low, lax.rsqrt(ms0 + NORM_EPS), lax.rsqrt(ms1 + NORM_EPS))
        o_ref[0, rows, :] = (o * inv * gain).astype(o_ref.dtype)

    for j in range(rows_per_step):
        row(j)


def _na_attn(qn, kn, vn, kn_c, vn_c, bias, out_gain, *, stabilise):
    b, s, width = qn.shape
    pairs = width // LANES
    rows_n = s // GRID_W
    rps = NA_ROWS_PER_STEP
    c = kn_c.shape[1]
    tq = rps * GRID_W
    kern = functools.partial(_na_attn_kernel, rows_n=rows_n, rows_per_step=rps, stabilise=stabilise)
    whole = lambda bi, h, i: (bi, 0, h)
    return pl.pallas_call(
        kern,
        out_shape=jax.ShapeDtypeStruct((b, s, width), jnp.bfloat16),
        grid=(b, pairs, rows_n // rps),
        in_specs=[pl.BlockSpec((1, tq, LANES), lambda bi, h, i: (bi, i, h)),
                  pl.BlockSpec((1, s, LANES), whole),
                  pl.BlockSpec((1, s, LANES), whole),
                  pl.BlockSpec((1, c, LANES), whole),
                  pl.BlockSpec((1, c, LANES), whole),
                  pl.BlockSpec((NA_ROWS, 1, 2 * GRID_W, NA_ROWS * GRID_W), lambda bi, h, i: (0, h, 0, 0)),
                  pl.BlockSpec((1, LANES), lambda bi, h, i: (0, h))],
        out_specs=pl.BlockSpec((1, tq, LANES), lambda bi, h, i: (bi, i, h)),
        scratch_shapes=[pltpu.VMEM((2, tq, LANES), jnp.float32), pltpu.VMEM((2, tq, LANES), jnp.float32)],
        compiler_params=pltpu.CompilerParams(vmem_limit_bytes=VMEM_LIMIT),
        name="na_attn",
    )(qn, kn, vn, kn_c, vn_c, bias, out_gain)


def _out_route_kernel(od_ref, on_ref, x_ref, w_ref, g1_ref, gf_ref, sh_ref, sc_ref, wr_ref, br_ref,
                      x1_ref, h2_ref, route_ref, gate_ref, cnt_ref):
    half = od_ref.shape[2]
    attn = (jnp.dot(od_ref[0], w_ref[:half], preferred_element_type=jnp.float32)
            + jnp.dot(on_ref[0], w_ref[half:], preferred_element_type=jnp.float32))
    x1 = x_ref[0] + g1_ref[0] * attn
    x1_ref[...] = x1
    y = x1 * lax.rsqrt(jnp.mean(x1 * x1, axis=-1, keepdims=True) + NORM_EPS) * gf_ref[...]
    h2 = y * (1.0 + sc_ref[0]) + sh_ref[0]
    h2_hi = h2.astype(jnp.bfloat16)
    h2_ref[...] = h2_hi

    h2_lo = (h2 - h2_hi.astype(jnp.float32)).astype(jnp.bfloat16)
    wr = wr_ref[...]
    wr_hi = wr.astype(jnp.bfloat16)
    wr_lo = (wr - wr_hi.astype(jnp.float32)).astype(jnp.bfloat16)
    logits = (jnp.dot(h2_hi, wr_hi, preferred_element_type=jnp.float32)
              + jnp.dot(h2_lo, wr_hi, preferred_element_type=jnp.float32)
              + jnp.dot(h2_hi, wr_lo, preferred_element_type=jnp.float32)) + br_ref[...]
    tm = logits.shape[0]
    lane = lax.broadcasted_iota(jnp.int32, logits.shape, 1)
    lane_f = lane.astype(jnp.float32)
    vals, idxs = [], []
    cur = logits
    for _ in range(TOP_K):
        mx = jnp.max(cur, axis=-1, keepdims=True)
        ik = jnp.min(jnp.where(cur == mx, lane_f, float(LANES)), axis=-1, keepdims=True)
        vals.append(mx)
        idxs.append(ik)
        cur = jnp.where(lane_f == ik, -3e38, cur)
    exps = [jnp.exp(v - vals[0]) for v in vals]
    denom = exps[0] + exps[1] + exps[2] + exps[3]

    onehot = jnp.zeros(logits.shape, jnp.float32)
    for ik in idxs:
        onehot = onehot + jnp.where(lane_f == ik, 1.0, 0.0)
    ti = lax.broadcasted_iota(jnp.int32, (tm, tm), 0)
    tj = lax.broadcasted_iota(jnp.int32, (tm, tm), 1)
    lower = jnp.where(tj < ti, 1.0, 0.0).astype(jnp.bfloat16)
    prefix = jnp.dot(lower, onehot.astype(jnp.bfloat16), preferred_element_type=jnp.float32)
    counts = jnp.sum(onehot, axis=0, keepdims=True)
    run = jnp.ceil(counts * (1.0 / RUN_ALIGN)) * RUN_ALIGN
    ei = lax.broadcasted_iota(jnp.int32, (LANES, LANES), 0)
    ej = lax.broadcasted_iota(jnp.int32, (LANES, LANES), 1)
    before = jnp.where(ei < ej, 1.0, 0.0)
    starts = jnp.dot(jnp.broadcast_to(run, (8, LANES)), before, precision=lax.Precision.HIGHEST,
                     preferred_element_type=jnp.float32)[0:1]
    slot = prefix + starts
    route = jnp.zeros(logits.shape, jnp.float32)
    gates = jnp.zeros(logits.shape, jnp.float32)
    for k in range(TOP_K):
        pos = jnp.sum(jnp.where(lane_f == idxs[k], slot, 0.0), axis=-1, keepdims=True)
        route = jnp.where(lane == k, idxs[k], route)
        route = jnp.where(lane == TOP_K + k, pos, route)
        gates = jnp.where(lane == k, exps[k] / denom, gates)
    route_ref[...] = route.astype(jnp.int32)
    gate_ref[...] = gates
    cnt_ref[0] = counts


def _out_route(o_diff, o_na, x, w_out, g1, g_ffn, sh2, sc2, w_router, b_router):
    b, s, d = x.shape
    tm = ROUTE_TM
    half = o_diff.shape[2]
    e = w_router.shape[1]
    wr = jnp.zeros((d, LANES), jnp.float32).at[:, :e].set(w_router)
    br = jnp.full((1, LANES), NEG, jnp.float32).at[0, :e].set(b_router)
    tile = lambda bi, i: (bi, i, 0)
    flat = lambda bi, i: (bi * (s // tm) + i, 0)
    row = lambda bi, i: (0, 0)
    per_b = lambda bi, i: (bi, 0, 0)
    return pl.pallas_call(
        _out_route_kernel,
        out_shape=[jax.ShapeDtypeStruct((b * s, d), jnp.float32),
                   jax.ShapeDtypeStruct((b * s, d), jnp.bfloat16),
                   jax.ShapeDtypeStruct((b * s, LANES), jnp.int32),
                   jax.ShapeDtypeStruct((b * s, LANES), jnp.float32),
                   jax.ShapeDtypeStruct((b * (s // tm), 1, LANES), jnp.float32)],
        grid=(b, s // tm),
        in_specs=[pl.BlockSpec((1, tm, half), tile),
                  pl.BlockSpec((1, tm, half), tile),
                  pl.BlockSpec((1, tm, d), tile),
                  pl.BlockSpec(w_out.shape, row),
                  pl.BlockSpec((1, 1, d), per_b),
                  pl.BlockSpec((1, d), row),
                  pl.BlockSpec((1, 1, d), per_b),
                  pl.BlockSpec((1, 1, d), per_b),
                  pl.BlockSpec((d, LANES), row),
                  pl.BlockSpec((1, LANES), row)],
        out_specs=[pl.BlockSpec((tm, d), flat),
                   pl.BlockSpec((tm, d), flat),
                   pl.BlockSpec((tm, LANES), flat),
                   pl.BlockSpec((tm, LANES), flat),
                   pl.BlockSpec((1, 1, LANES), lambda bi, i: (bi * (s // tm) + i, 0, 0))],
        compiler_params=pltpu.CompilerParams(vmem_limit_bytes=VMEM_LIMIT),
        name="out_route",
    )(o_diff, o_na, x, w_out, g1, g_ffn, sh2, sc2, wr, br)


def _chunk_copies(tile, used_ref, dest_ref, local_ref, global_ref, sem, *, to_global, wait):
    n_chunks = local_ref.shape[0] // RUN_ALIGN

    def one(c, carry):
        lref = local_ref.at[pl.ds(pl.multiple_of(c * RUN_ALIGN, RUN_ALIGN), RUN_ALIGN), :]
        gref = global_ref.at[pl.ds(pl.multiple_of(dest_ref[tile * n_chunks + c], RUN_ALIGN), RUN_ALIGN), :]
        cp = pltpu.make_async_copy(lref, gref, sem) if to_global else pltpu.make_async_copy(gref, lref, sem)
        cp.wait() if wait else cp.start()
        return carry

    def group(g, carry):
        for u in range(CHUNK_UNROLL):
            one(g * CHUNK_UNROLL + u, carry)
        return carry

    used = used_ref[tile]
    n_groups = used // CHUNK_UNROLL
    lax.fori_loop(0, n_groups, group, 0)
    lax.fori_loop(n_groups * CHUNK_UNROLL, used, one, 0)


def _slot_matrix(route, values, n_slots):
    width = 2 * LANES
    blocks = []
    for first in range(0, n_slots, width):
        slots = lax.broadcasted_iota(jnp.int32, (route.shape[0], width), 1) + first
        block = jnp.zeros(slots.shape, jnp.float32)
        for k in range(TOP_K):
            block = jnp.where(slots == route[:, TOP_K + k:TOP_K + k + 1], values[k], block)
        blocks.append(block.astype(jnp.bfloat16))
    return jnp.concatenate(blocks, axis=1)


def _zero_unsorted_rows(pad_row_ref, pad_chunks_ref, nv_ref, zero_ref, xs_ref, sem):
    zero_ref[...] = jnp.zeros(zero_ref.shape, zero_ref.dtype)
    n_tiles = xs_ref.shape[0] // EXPERT_TM
    for wait in (False, True):
        def pad_chunk(e, j, c):
            row = pl.multiple_of(pad_row_ref[e] + j * RUN_ALIGN, RUN_ALIGN)
            cp = pltpu.make_async_copy(zero_ref.at[pl.ds(0, RUN_ALIGN), :], xs_ref.at[pl.ds(row, RUN_ALIGN), :], sem)
            cp.wait() if wait else cp.start()
            return c

        def unused_tile(i, c):
            row = pl.multiple_of(i * EXPERT_TM, EXPERT_TM)
            cp = pltpu.make_async_copy(zero_ref, xs_ref.at[pl.ds(row, EXPERT_TM), :], sem)
            cp.wait() if wait else cp.start()
            return c

        lax.fori_loop(0, N_EXPERTS, lambda e, c: lax.fori_loop(
            0, pad_chunks_ref[e], functools.partial(pad_chunk, e), c), 0)
        lax.fori_loop(nv_ref[0], n_tiles, unused_tile, 0)


def _dispatch_kernel(used_ref, dest_ref, pad_row_ref, pad_chunks_ref, nv_ref, route_ref, h_ref, xs_ref,
                     buf_ref, zero_ref, sems, zero_sem):
    tile = pl.program_id(0)
    slot = tile % 2

    @pl.when(tile == 0)
    def _():
        _zero_unsorted_rows(pad_row_ref, pad_chunks_ref, nv_ref, zero_ref, xs_ref, zero_sem)

    route = route_ref[...]
    n_slots = buf_ref.shape[1]
    perm = _slot_matrix(route, (1.0,) * TOP_K, n_slots)
    rows = lax.dot_general(perm, h_ref[...], (((0,), (0,)), ((), ())),
                           preferred_element_type=jnp.float32)
    buf_ref[slot] = rows.astype(buf_ref.dtype)

    def copies(t, s, wait):
        _chunk_copies(t, used_ref, dest_ref, buf_ref.at[s], xs_ref, sems.at[s], to_global=True, wait=wait)

    copies(tile, slot, False)

    @pl.when(tile > 0)
    def _():
        copies(tile - 1, 1 - slot, True)

    @pl.when(tile == pl.num_programs(0) - 1)
    def _():
        copies(tile, slot, True)


def _dispatch(used, dest, pad_row, pad_chunks, n_valid, route, h2, n_rows):
    t, d = h2.shape
    tm = ROUTE_TM
    grid_spec = pltpu.PrefetchScalarGridSpec(
        num_scalar_prefetch=5,
        grid=(t // tm,),
        in_specs=[pl.BlockSpec((tm, LANES), lambda i, *_: (i, 0)),
                  pl.BlockSpec((tm, d), lambda i, *_: (i, 0))],
        out_specs=pl.BlockSpec(memory_space=pl.ANY),
        scratch_shapes=[pltpu.VMEM((2, TILE_SLOTS, d), h2.dtype), pltpu.VMEM((EXPERT_TM, d), h2.dtype),
                        pltpu.SemaphoreType.DMA((2,)), pltpu.SemaphoreType.DMA(())],
    )
    return pl.pallas_call(
        _dispatch_kernel,
        out_shape=jax.ShapeDtypeStruct((n_rows, d), h2.dtype),
        grid_spec=grid_spec,
        compiler_params=pltpu.CompilerParams(vmem_limit_bytes=VMEM_LIMIT),
        name="dispatch",
    )(used, dest, pad_row, pad_chunks, n_valid, route, h2)


def _experts_kernel(be_ref, nv_ref, xs_ref, wgu_ref, bgu_ref, wdn_ref, bdn_ref, ys_ref, wgu_b_ref, wdn_b_ref):
    i = pl.program_id(0)
    valid = i < nv_ref[0]
    new_expert = (i == 0) | (be_ref[i] != be_ref[jnp.maximum(i - 1, 0)])

    @pl.when(valid & new_expert)
    def _():
        wgu_b_ref[...] = wgu_ref[0].astype(jnp.bfloat16)
        wdn_b_ref[...] = wdn_ref[0].astype(jnp.bfloat16)

    @pl.when(valid)
    def _():
        gu = jnp.dot(xs_ref[...], wgu_b_ref[...], preferred_element_type=jnp.float32) + bgu_ref[0]
        f = gu.shape[1] // 2
        glu = jnp.minimum(gu[:, :f], SWIGLU_LIMIT)
        lin = jnp.clip(gu[:, f:], -SWIGLU_LIMIT, SWIGLU_LIMIT)
        act = glu * (1.0 / (1.0 + jnp.exp(-SWIGLU_ALPHA * glu))) * (lin + 1.0)
        y = jnp.dot(act.astype(jnp.bfloat16), wdn_b_ref[...], preferred_element_type=jnp.float32) + bdn_ref[0]
        ys_ref[...] = y.astype(ys_ref.dtype)

    @pl.when(jnp.logical_not(valid))
    def _():
        ys_ref[...] = jnp.zeros(ys_ref.shape, ys_ref.dtype)


def _experts(block_expert, n_valid, xs, w_gu, b_gu, w_dn, b_dn):
    n_rows, dw = xs.shape
    tm = EXPERT_TM
    e, d, f2 = w_gu.shape
    grid_spec = pltpu.PrefetchScalarGridSpec(
        num_scalar_prefetch=2,
        grid=(n_rows // tm,),
        in_specs=[pl.BlockSpec((tm, dw), lambda i, be, nv: (jnp.clip(i, 0, jnp.maximum(nv[0] - 1, 0)), 0)),
                  pl.BlockSpec((1, d, f2), lambda i, be, nv: (be[i], 0, 0)),
                  pl.BlockSpec((1, 1, f2), lambda i, be, nv: (be[i], 0, 0)),
                  pl.BlockSpec((1, f2 // 2, d), lambda i, be, nv: (be[i], 0, 0)),
                  pl.BlockSpec((1, 1, d), lambda i, be, nv: (be[i], 0, 0))],
        out_specs=pl.BlockSpec((tm, d), lambda i, be, nv: (i, 0)),
        scratch_shapes=[pltpu.VMEM((d, f2), jnp.bfloat16), pltpu.VMEM((f2 // 2, d), jnp.bfloat16)],
    )
    return pl.pallas_call(
        _experts_kernel,
        out_shape=jax.ShapeDtypeStruct((n_rows, d), jnp.bfloat16),
        grid_spec=grid_spec,
        compiler_params=pltpu.CompilerParams(vmem_limit_bytes=EXPERTS_VMEM_LIMIT),
        name="experts",
    )(block_expert, n_valid, xs, w_gu, b_gu.reshape(e, 1, f2), w_dn, b_dn.reshape(e, 1, d))


def _combine_kernel(used_ref, dest_ref, route_ref, gate_ref, x1_ref, g2_ref, ys_ref, o_ref, buf_ref, sems):
    tile = pl.program_id(0)
    slot = tile % 2
    n_slots = buf_ref.shape[1]
    covered = route_ref.shape[0] * TOP_K

    def copies(t, s, wait):
        _chunk_copies(t, used_ref, dest_ref, buf_ref.at[s], ys_ref, sems.at[s], to_global=False, wait=wait)

    def fetch(t, s):
        buf_ref[s, covered:] = jnp.zeros((n_slots - covered, buf_ref.shape[2]), buf_ref.dtype)
        copies(t, s, False)

    @pl.when(tile == 0)
    def _():
        fetch(0, 0)

    @pl.when(tile + 1 < pl.num_programs(0))
    def _():
        fetch(tile + 1, 1 - slot)

    route = route_ref[...]
    gates = gate_ref[...]
    copies(tile, slot, True)
    weights = _slot_matrix(route, [gates[:, k:k + 1] for k in range(TOP_K)], n_slots)
    moe = jnp.dot(weights, buf_ref[slot], preferred_element_type=jnp.float32)
    o_ref[...] = x1_ref[...] + g2_ref[0] * moe


def _combine(used, dest, route, gates, x1, g2, ys, tiles_per_batch):
    t, d = x1.shape
    tm = ROUTE_TM
    grid_spec = pltpu.PrefetchScalarGridSpec(
        num_scalar_prefetch=2,
        grid=(t // tm,),
        in_specs=[pl.BlockSpec((tm, LANES), lambda i, *_: (i, 0)),
                  pl.BlockSpec((tm, LANES), lambda i, *_: (i, 0)),
                  pl.BlockSpec((tm, d), lambda i, *_: (i, 0)),
                  pl.BlockSpec((1, 1, d), lambda i, *_: (i // tiles_per_batch, 0, 0)),
                  pl.BlockSpec(memory_space=pl.ANY)],
        out_specs=pl.BlockSpec((tm, d), lambda i, *_: (i, 0)),
        scratch_shapes=[pltpu.VMEM((2, TILE_SLOTS, d), ys.dtype), pltpu.SemaphoreType.DMA((2,))],
    )
    return pl.pallas_call(
        _combine_kernel,
        out_shape=jax.ShapeDtypeStruct((t, d), jnp.float32),
        grid_spec=grid_spec,
        compiler_params=pltpu.CompilerParams(vmem_limit_bytes=VMEM_LIMIT),
        name="combine",
    )(used, dest, route, gates, x1, g2, ys)


def _score_bound(q_gain, k_gain):
    rounding = 1.02
    return HEAD_DIM * jnp.max(jnp.abs(q_gain)) * jnp.max(jnp.abs(k_gain)) * Q_SCALE * rounding

def _rope_tables(s):
    pos = jnp.arange(s, dtype=jnp.int32)
    inv = 1.0 / (ROPE_BASE ** (jnp.arange(ROPE_FREQS, dtype=jnp.float32) / ROPE_FREQS))
    ang_r = (pos // GRID_W).astype(jnp.float32)[:, None] * inv
    ang_c = (pos % GRID_W).astype(jnp.float32)[:, None] * inv
    ang = jnp.concatenate([ang_r, ang_r, ang_c, ang_c], axis=-1)
    sign = jnp.asarray(np.tile(np.repeat([-1.0, 1.0], ROPE_FREQS), 2), jnp.float32)
    cos_t = jnp.tile(jnp.cos(ang), (1, LANES // HEAD_DIM))
    sin_t = jnp.tile(jnp.sin(ang) * sign, (1, LANES // HEAD_DIM))
    return cos_t, sin_t


def kernel(x, c, ctx, c_ctx, w_ada, b_ada, g_attn, w_in, q_norm_diff, k_norm_diff, lam_q1, lam_k1, lam_q2,
           lam_k2, subln_diff, q_norm_na, k_norm_na, rpb_na, out_norm_na, w_out, g_ffn, w_router, b_router,
           w_gate_up, b_gate_up, w_down, b_down):
    depth = w_ada.shape[0]
    assert depth == 1, "single-layer kernel"
    b, s, d = x.shape
    n_ctx = ctx.shape[1]
    rows_n = s // GRID_W
    assert rows_n >= NA_ROWS and s % PROJ_TM == 0 and rows_n % NA_ROWS_PER_STEP == 0
    lam_init = 0.8 - 0.6 * math.exp(-0.3 * 0)

    rows = -(-(b + 1) // 8) * 8
    cv = jnp.zeros((rows, d), jnp.float32).at[:b].set(c).at[b].set(c_ctx)
    mod = _adaln(cv, w_ada[0], b_ada[0]).reshape(rows, 6, d)
    lat = [mod[:b, i][:, None, :] for i in range(6)]
    cxm = [jnp.broadcast_to(mod[b, i][None, None, :], (b, 1, d)) for i in range(6)]
    sh1, sc1, g1, sh2, sc2, g2 = lat

    w_in_b = w_in[0].astype(jnp.bfloat16)
    gidx = np.arange(256) // HEAD_DIM
    gsum = jnp.asarray(gidx[:, None] == gidx[None, :], jnp.bfloat16)
    tile4 = lambda v, reps: jnp.tile(v.reshape(1, -1), (1, reps))
    norms = (tile4(q_norm_diff[0], 8), tile4(k_norm_diff[0], 8), tile4(q_norm_na[0], 8), tile4(k_norm_na[0], 8))
    cos_t, sin_t = _rope_tables(s)
    g_attn2 = g_attn[0].reshape(1, d)

    qd, kd, vdt, qn, kn, vn = _proj(x, g_attn2, sh1, sc1, w_in_b, gsum, cos_t, sin_t, norms,
                                    groups=("qd", "kd", "vd", "qn", "kn", "vn"), rope=True, tm=PROJ_TM)
    kd_c, vdt_c, kn_c, vn_c = _proj(ctx, g_attn2, cxm[0], cxm[1], w_in_b, gsum, cos_t[:n_ctx], sin_t[:n_ctx],
                                    norms, groups=("kd", "vd", "kn", "vn"), rope=False, tm=n_ctx)

    lams = tuple(v[0].reshape(1, HEAD_DIM) for v in (lam_q1, lam_k1, lam_q2, lam_k2))
    diff_args = (qd, kd, kd_c, vdt, vdt_c, lams, subln_diff[0].reshape(1, LANES))
    o_diff = lax.cond(
        _score_bound(q_norm_diff[0], k_norm_diff[0]) <= EXP2_SAFE_SCORE,
        lambda a: _diff_attn(*a, lam_init=lam_init, stabilise=False),
        lambda a: _diff_attn(*a, lam_init=lam_init, stabilise=True), diff_args)
    na_args = (qn, kn, vn, kn_c, vn_c, _na_bias_table(rpb_na[0]), out_norm_na[0].reshape(1, -1))
    na_bound = _score_bound(q_norm_na[0], k_norm_na[0]) + jnp.max(jnp.abs(rpb_na[0])) * LOG2E
    o_na = lax.cond(
        na_bound <= EXP2_SAFE_SCORE,
        lambda a: _na_attn(*a, stabilise=False),
        lambda a: _na_attn(*a, stabilise=True), na_args)

    x1, h2, route, gates, counts = _out_route(o_diff, o_na, x, w_out[0].astype(jnp.bfloat16), g1,
                                              g_ffn[0].reshape(1, d), sh2, sc2, w_router[0], b_router[0])

    t = b * s
    n_exp = w_router.shape[2]
    assert n_exp == N_EXPERTS
    tile_cnt = counts[:, 0, :n_exp].astype(jnp.int32)
    tile_run = (tile_cnt + RUN_ALIGN - 1) // RUN_ALIGN * RUN_ALIGN
    totals = jnp.sum(tile_run, axis=0)
    padded = (totals + EXPERT_TM - 1) // EXPERT_TM * EXPERT_TM
    pad_ends = jnp.cumsum(padded)
    pad_starts = pad_ends - padded
    tile_loc = jnp.cumsum(tile_run, axis=1) - tile_run
    tile_glob = pad_starts[None, :] + jnp.cumsum(tile_run, axis=0) - tile_run
    max_rows = t * TOP_K + tile_cnt.size * (RUN_ALIGN - 1)
    n_tiles = -(-max_rows // EXPERT_TM) + n_exp
    tile_row0 = jnp.arange(n_tiles, dtype=jnp.int32) * EXPERT_TM
    block_expert = jnp.minimum(jnp.sum(tile_row0[:, None] >= pad_ends[None, :], axis=1), n_exp - 1).astype(jnp.int32)
    n_valid = (pad_ends[-1:] // EXPERT_TM).astype(jnp.int32)
    run_end = tile_loc + tile_run
    chunk_row = (jnp.arange(TILE_SLOTS // RUN_ALIGN, dtype=jnp.int32) * RUN_ALIGN)[None, :, None]
    in_run = (chunk_row >= tile_loc[:, None, :]) & (chunk_row < run_end[:, None, :])
    chunk_dest = jnp.sum(jnp.where(in_run, (tile_glob - tile_loc)[:, None, :], 0), axis=-1) + chunk_row[:, :, 0]
    chunks_used = run_end[:, -1] // RUN_ALIGN
    moves = (chunks_used.astype(jnp.int32), chunk_dest.reshape(-1).astype(jnp.int32))
    pad_row = (pad_starts + totals).astype(jnp.int32)
    pad_chunks = ((padded - totals) // RUN_ALIGN).astype(jnp.int32)

    xs = _dispatch(*moves, pad_row, pad_chunks, n_valid, route, h2, n_tiles * EXPERT_TM)
    ys = _experts(block_expert, n_valid, xs, w_gate_up[0], b_gate_up[0], w_down[0], b_down[0])
    out = _combine(*moves, route, gates, x1, g2, ys, s // ROUTE_TM)
    return out.reshape(b, s, d)
```
